```python
import jax
import jax.numpy as jnp
from jax import lax
import numpy as np


D_MODEL = 1024
BATCH = 2
SEQ = 16384
DEPTH = 2
DEC_BATCH = 4
DEC_SEQ = 4096
PAST_LEN = 128

N_META = 16
LRU_WIDTH = 512
LRU_BLOCKS = 8
LRU_BLOCK = LRU_WIDTH // LRU_BLOCKS
CONV_WIDTH = 4
CONV_LEFT = 2
CONV_RIGHT = CONV_WIDTH - 1 - CONV_LEFT
LRU_C = 8.0
MLA_HEADS = 8
Q_LORA = 256
KV_LORA = 128
QK_NOPE = 64
QK_ROPE = 32
QK_HEAD = QK_NOPE + QK_ROPE
V_HEAD = 64
ATTN_WIDTH = MLA_HEADS * V_HEAD
ROPE_THETA = 10000.0
Q_BLOCK = 128
MIX_WIDTH = LRU_WIDTH + ATTN_WIDTH
OFF_U = 0
OFF_GATE = OFF_U + LRU_WIDTH
OFF_CQ = OFF_GATE + LRU_WIDTH
OFF_CKV = OFF_CQ + Q_LORA
OFF_KR = OFF_CKV + KV_LORA
IN_WIDTH = OFF_KR + QK_ROPE
D_FF = 4 * D_MODEL
EPS = 1e-6

kernel_name = 'hymba_rglru_mla_bidir_encoder'


def rmsnorm(x, g):
    xf = x.astype(jnp.float32)
    y = xf * lax.rsqrt(jnp.mean(xf * xf, axis=-1, keepdims=True) + EPS)
    return (y * g.astype(jnp.float32)).astype(x.dtype)


def rope_tables(T):
    inv_freq = 1.0 / (ROPE_THETA ** (jnp.arange(0, QK_ROPE, 2, dtype=jnp.float32) / QK_ROPE))
    ang = jnp.arange(T, dtype=jnp.float32)[:, None] * inv_freq[None, :]
    return jnp.cos(ang)[:, None, :], jnp.sin(ang)[:, None, :]


def apply_rope(x, cos, sin):
    xf = x.astype(jnp.float32)
    x_nope = xf[..., :QK_NOPE]
    x1 = xf[..., QK_NOPE:QK_NOPE + QK_ROPE // 2]
    x2 = xf[..., QK_NOPE + QK_ROPE // 2:]
    out = jnp.concatenate([x_nope, x1 * cos - x2 * sin, x2 * cos + x1 * sin], axis=-1)
    return out.astype(x.dtype)


def linear_scan_combine(left, right):
    a_l, b_l = left
    a_r, b_r = right
    return a_l * a_r, a_r * b_l + b_r


def rglru_group(u, gate, conv_w, conv_b, wa, ba, wx, bx, lam):
    B, T, _ = u.shape
    up = jnp.pad(u, ((0, 0), (CONV_LEFT, CONV_RIGHT), (0, 0)))
    xc = conv_b.astype(u.dtype)
    for tap in range(CONV_WIDTH):
        xc = xc + up[:, tap:tap + T] * conv_w[tap]
    xf = xc.astype(jnp.float32)
    xb = xf.reshape(B, T, LRU_BLOCKS, LRU_BLOCK)
    r = jax.nn.sigmoid(jnp.einsum('btgi,zgij->zbtgj', xb, wa.astype(jnp.float32)).reshape(2, B, T, LRU_WIDTH)
                       + ba.astype(jnp.float32)[:, None, None, :])
    ig = jax.nn.sigmoid(jnp.einsum('btgi,zgij->zbtgj', xb, wx.astype(jnp.float32)).reshape(2, B, T, LRU_WIDTH)
                        + bx.astype(jnp.float32)[:, None, None, :])
    log_a = -LRU_C * r * jax.nn.softplus(-lam.astype(jnp.float32))[:, None, None, :]
    a = jnp.exp(log_a)
    b = jnp.sqrt(-jnp.expm1(2.0 * log_a)) * (ig * xf[None])
    _, h_fwd = lax.associative_scan(linear_scan_combine, (a[0], b[0]), axis=1)
    _, h_bwd = lax.associative_scan(linear_scan_combine, (a[1], b[1]), axis=1, reverse=True)
    y = (h_fwd + h_bwd) * jax.nn.gelu(gate.astype(jnp.float32))
    return y.astype(u.dtype)


def block_attention(q, k, v):
    B, T, H, _ = q.shape
    nblk = -(-T // Q_BLOCK)
    pad = nblk * Q_BLOCK - T
    qp = jnp.pad(q, ((0, 0), (0, pad), (0, 0), (0, 0)))
    qb = qp.reshape(B, nblk, Q_BLOCK, H, QK_HEAD).transpose(1, 0, 2, 3, 4)
    scale = QK_HEAD ** -0.5

    def one_block(q_blk):
        s = jnp.einsum('bqhd,bkhd->bhqk', q_blk, k, preferred_element_type=jnp.float32) * scale
        p = jax.nn.softmax(s, axis=-1)
        return jnp.einsum('bhqk,bkhd->bqhd', p.astype(v.dtype), v)

    o = lax.map(one_block, qb)
    return o.transpose(1, 0, 2, 3, 4).reshape(B, nblk * Q_BLOCK, H, V_HEAD)[:, :T]


def mla_group(c_q, c_kv, k_rope, cos, sin, q_norm_g, w_uq, kv_norm_g, w_ukv, qk_q_g, qk_k_g):
    B, T, _ = c_q.shape
    q = (rmsnorm(c_q, q_norm_g) @ w_uq).reshape(B, T, MLA_HEADS, QK_HEAD)
    kv = (rmsnorm(c_kv, kv_norm_g) @ w_ukv).reshape(B, T, MLA_HEADS, QK_NOPE + V_HEAD)
    k_nope, v = kv[..., :QK_NOPE], kv[..., QK_NOPE:]
    k = jnp.concatenate([k_nope, jnp.broadcast_to(k_rope[:, :, None, :], (B, T, MLA_HEADS, QK_ROPE))], axis=-1)
    q = apply_rope(rmsnorm(q, qk_q_g), cos, sin)
    k = apply_rope(rmsnorm(k, qk_k_g), cos, sin)
    o = block_attention(q, k, v)
    return o.reshape(B, T, ATTN_WIDTH)


def run_trunk(x, meta_tokens, norm_mix_g, w_in, conv_w, conv_b, lru_wa, lru_ba, lru_wx, lru_bx, lru_lambda,
              q_norm_g, w_uq, kv_norm_g, w_ukv, qk_q_g, qk_k_g, out_norm_lru_g, out_norm_attn_g, w_out,
              norm_ff_g, w_up, w_down):
    B, S, _ = x.shape
    T = S + N_META
    meta = jnp.broadcast_to(meta_tokens.astype(x.dtype)[None], (B, N_META, D_MODEL))
    h = jnp.concatenate([meta, x], axis=1)
    cos, sin = rope_tables(T)
    for l in range(DEPTH):
        hn = rmsnorm(h, norm_mix_g[l])
        proj = hn @ w_in[l]
        u = proj[..., OFF_U:OFF_U + LRU_WIDTH]
        gate = proj[..., OFF_GATE:OFF_GATE + LRU_WIDTH]
        c_q = proj[..., OFF_CQ:OFF_CQ + Q_LORA]
        c_kv = proj[..., OFF_CKV:OFF_CKV + KV_LORA]
        k_rope = proj[..., OFF_KR:OFF_KR + QK_ROPE]
        y_lru = rglru_group(u, gate, conv_w[l], conv_b[l], lru_wa[l], lru_ba[l], lru_wx[l], lru_bx[l], lru_lambda[l])
        y_att = mla_group(c_q, c_kv, k_rope, cos, sin, q_norm_g[l], w_uq[l], kv_norm_g[l], w_ukv[l],
                          qk_q_g[l], qk_k_g[l])
        mixed = jnp.concatenate([rmsnorm(y_lru, out_norm_lru_g[l]), rmsnorm(y_att, out_norm_attn_g[l])], axis=-1)
        h = h + mixed @ w_out[l]
        hf = rmsnorm(h, norm_ff_g[l])
        h = h + jnp.square(jax.nn.relu(hf @ w_up[l])) @ w_down[l]
    return h[:, N_META:]


def setup_inputs(seed: int = 0) -> dict:
    key = jax.random.key(seed)
    ks = jax.random.split(key, 24)

    def nrm(k, shape, scale):
        return jax.random.normal(k, shape, jnp.float32) * scale

    x_prompt = nrm(ks[0], (BATCH, SEQ, D_MODEL), 1.0)
    x_sample = nrm(ks[1], (DEC_BATCH, DEC_SEQ, D_MODEL), 1.0)
    meta_tokens = nrm(ks[2], (N_META, D_MODEL), 1.0)
    norm_mix_g = 1.0 + nrm(ks[3], (DEPTH, D_MODEL), 0.02)
    w_in = nrm(ks[4], (DEPTH, D_MODEL, IN_WIDTH), D_MODEL ** -0.5)
    conv_w = nrm(ks[5], (DEPTH, CONV_WIDTH, LRU_WIDTH), CONV_WIDTH ** -0.5)
    conv_b = nrm(ks[6], (DEPTH, LRU_WIDTH), 0.02)
    lru_wa = nrm(ks[7], (DEPTH, 2, LRU_BLOCKS, LRU_BLOCK, LRU_BLOCK), LRU_BLOCK ** -0.5)
    lru_ba = nrm(ks[8], (DEPTH, 2, LRU_WIDTH), 0.1)
    lru_wx = nrm(ks[9], (DEPTH, 2, LRU_BLOCKS, LRU_BLOCK, LRU_BLOCK), LRU_BLOCK ** -0.5)
    lru_bx = nrm(ks[10], (DEPTH, 2, LRU_WIDTH), 0.1)
    a_c = jax.random.uniform(ks[11], (DEPTH, 2, LRU_WIDTH), jnp.float32, 0.9, 0.999)
    a0 = a_c ** (1.0 / LRU_C)
    lru_lambda = jnp.log(a0) - jnp.log1p(-a0)
    q_norm_g = 1.0 + nrm(ks[12], (DEPTH, Q_LORA), 0.02)
    w_uq = nrm(ks[13], (DEPTH, Q_LORA, MLA_HEADS * QK_HEAD), Q_LORA ** -0.5)
    kv_norm_g = 1.0 + nrm(ks[14], (DEPTH, KV_LORA), 0.02)
    w_ukv = nrm(ks[15], (DEPTH, KV_LORA, MLA_HEADS * (QK_NOPE + V_HEAD)), KV_LORA ** -0.5)
    qk_q_g = 1.0 + nrm(ks[16], (DEPTH, QK_HEAD), 0.02)
    qk_k_g = 1.0 + nrm(ks[17], (DEPTH, QK_HEAD), 0.02)
    out_norm_lru_g = 1.0 + nrm(ks[18], (DEPTH, LRU_WIDTH), 0.02)
    out_norm_attn_g = 1.0 + nrm(ks[19], (DEPTH, ATTN_WIDTH), 0.02)
    w_out = nrm(ks[20], (DEPTH, MIX_WIDTH, D_MODEL), MIX_WIDTH ** -0.5)
    norm_ff_g = 1.0 + nrm(ks[21], (DEPTH, D_MODEL), 0.02)
    w_up = nrm(ks[22], (DEPTH, D_MODEL, D_FF), D_MODEL ** -0.5)
    w_down = nrm(ks[23], (DEPTH, D_FF, D_MODEL), D_FF ** -0.5)
    return {'x_prompt': x_prompt, 'x_sample': x_sample, 'meta_tokens': meta_tokens,
            'norm_mix_g': norm_mix_g, 'w_in': w_in, 'conv_w': conv_w, 'conv_b': conv_b,
            'lru_wa': lru_wa, 'lru_ba': lru_ba, 'lru_wx': lru_wx, 'lru_bx': lru_bx, 'lru_lambda': lru_lambda,
            'q_norm_g': q_norm_g, 'w_uq': w_uq, 'kv_norm_g': kv_norm_g, 'w_ukv': w_ukv,
            'qk_q_g': qk_q_g, 'qk_k_g': qk_k_g, 'out_norm_lru_g': out_norm_lru_g,
            'out_norm_attn_g': out_norm_attn_g, 'w_out': w_out, 'norm_ff_g': norm_ff_g,
            'w_up': w_up, 'w_down': w_down}


def reference(x_prompt, x_sample, meta_tokens, norm_mix_g, w_in, conv_w, conv_b, lru_wa, lru_ba, lru_wx, lru_bx,
              lru_lambda, q_norm_g, w_uq, kv_norm_g, w_ukv, qk_q_g, qk_k_g, out_norm_lru_g, out_norm_attn_g,
              w_out, norm_ff_g, w_up, w_down):
    y_prompt = run_trunk(x_prompt, meta_tokens, norm_mix_g, w_in, conv_w, conv_b, lru_wa, lru_ba, lru_wx, lru_bx,
                         lru_lambda, q_norm_g, w_uq, kv_norm_g, w_ukv, qk_q_g, qk_k_g, out_norm_lru_g,
                         out_norm_attn_g, w_out, norm_ff_g, w_up, w_down)
    y_sample = run_trunk(x_sample, meta_tokens, norm_mix_g, w_in, conv_w, conv_b, lru_wa, lru_ba, lru_wx, lru_bx,
                         lru_lambda, q_norm_g, w_uq, kv_norm_g, w_ukv, qk_q_g, qk_k_g, out_norm_lru_g,
                         out_norm_attn_g, w_out, norm_ff_g, w_up, w_down)
    return (y_prompt, y_sample)
```

```python
import functools

import jax
import jax.numpy as jnp
from jax import lax
from jax.experimental import pallas as pl
from jax.experimental.pallas import tpu as pltpu

N_META = 16
LRU_WIDTH = 512
LRU_BLOCKS = 8
LRU_BLOCK = LRU_WIDTH // LRU_BLOCKS
CONV_WIDTH = 4
CONV_LEFT = 2
LRU_C = 8.0
MLA_HEADS = 8
Q_LORA = 256
KV_LORA = 128
QK_NOPE = 64
QK_ROPE = 32
QK_HEAD = QK_NOPE + QK_ROPE
V_HEAD = 64
ATTN_WIDTH = MLA_HEADS * V_HEAD
ROPE_THETA = 10000.0
EPS = 1e-6

LANES = 128
SUBLANES = 8
MXU_DIM = 256
VMEM_LIMIT_BYTES = 56 * 1024 * 1024

HEAD_PAD = LANES
ROW_TILE = 512
KEY_CHUNK = 1024
HEADS_PER_STEP = HEAD_PAD // V_HEAD
GATE_GROUP = MXU_DIM

F32 = jnp.float32
BF16 = jnp.bfloat16


def _rmsnorm(x, g):
    return x * lax.rsqrt(jnp.mean(x * x, axis=-1, keepdims=True) + EPS) * g


def _dot(a, b):
    return jnp.dot(a, b, preferred_element_type=F32)


def _head_norm_rope(x, g, cos, sin_a, sin_b):
    ss = jnp.sum(x * x, axis=-1, keepdims=True) * (1.0 / QK_HEAD)
    y = x * lax.rsqrt(ss + EPS) * g
    hi_to_lo = pltpu.roll(y, HEAD_PAD - QK_ROPE // 2, axis=1)
    lo_to_hi = pltpu.roll(y, QK_ROPE // 2, axis=1)
    return y * cos + hi_to_lo * sin_a + lo_to_hi * sin_b


def _in_proj_kernel(h_ref, gmix_ref, win_ref, gqn_ref, wuq_ref, gkvn_ref, wuk_ref, wuv_ref,
                    gq_ref, gk_ref, cos_ref, sina_ref, sinb_ref,
                    u_ref, gate_ref, q_ref, k_ref, v_ref):
    hn = _rmsnorm(h_ref[0], gmix_ref[...])
    proj = _dot(hn.astype(BF16), win_ref[...])
    o_gate = LRU_WIDTH
    o_cq = 2 * LRU_WIDTH
    o_ckv = o_cq + Q_LORA
    o_kr = o_ckv + KV_LORA
    u_ref[0] = proj[:, :o_gate]
    gate_ref[0] = proj[:, o_gate:o_cq]
    k_rope = proj[:, o_kr:o_kr + HEAD_PAD]
    qn = _rmsnorm(proj[:, o_cq:o_ckv], gqn_ref[...]).astype(BF16)
    kvn = _rmsnorm(proj[:, o_ckv:o_kr], gkvn_ref[...]).astype(BF16)
    qf = _dot(qn, wuq_ref[...])
    kf = _dot(kvn, wuk_ref[...])
    v_ref[0] = _dot(kvn, wuv_ref[...]).astype(BF16)
    cos, sin_a, sin_b = cos_ref[...], sina_ref[...], sinb_ref[...]
    gq, gk = gq_ref[...], gk_ref[...]
    scale = QK_HEAD ** -0.5
    for hd in range(MLA_HEADS):
        sl = slice(hd * HEAD_PAD, (hd + 1) * HEAD_PAD)
        q = _head_norm_rope(qf[:, sl], gq, cos, sin_a, sin_b)
        q_ref[0, hd] = (q * scale).astype(BF16)
        k = _head_norm_rope(kf[:, sl] + k_rope, gk, cos, sin_a, sin_b)
        k_ref[0, hd] = k.astype(BF16)


def _in_proj(h, lw, tabs):
    B, Tp, D = h.shape
    nT = Tp // ROW_TILE
    const = lambda b, j: (0, 0)
    row3 = lambda b, j: (b, j, 0)
    full = lambda a: pl.BlockSpec(a.shape, const)
    weights = [lw['gmix'], lw['w_in'], lw['gqn'], lw['w_uq'], lw['gkvn'], lw['w_uk'], lw['w_uv'],
               lw['gq'], lw['gk']]
    tab_spec = pl.BlockSpec((ROW_TILE, HEAD_PAD), lambda b, j: (j, 0))
    head_spec = pl.BlockSpec((1, MLA_HEADS, ROW_TILE, HEAD_PAD), lambda b, j: (b, 0, j, 0))
    return pl.pallas_call(
        _in_proj_kernel,
        grid=(B, nT),
        in_specs=[pl.BlockSpec((1, ROW_TILE, D), row3)] + [full(w) for w in weights] + [tab_spec] * 3,
        out_specs=[pl.BlockSpec((1, ROW_TILE, LRU_WIDTH), row3),
                   pl.BlockSpec((1, ROW_TILE, LRU_WIDTH), row3),
                   head_spec, head_spec,
                   pl.BlockSpec((1, ROW_TILE, ATTN_WIDTH), row3)],
        out_shape=[jax.ShapeDtypeStruct((B, Tp, LRU_WIDTH), F32),
                   jax.ShapeDtypeStruct((B, Tp, LRU_WIDTH), F32),
                   jax.ShapeDtypeStruct((B, MLA_HEADS, Tp, HEAD_PAD), BF16),
                   jax.ShapeDtypeStruct((B, MLA_HEADS, Tp, HEAD_PAD), BF16),
                   jax.ShapeDtypeStruct((B, Tp, ATTN_WIDTH), BF16)],
        compiler_params=pltpu.CompilerParams(
            dimension_semantics=("parallel", "parallel"), vmem_limit_bytes=VMEM_LIMIT_BYTES),
        name="in_proj",
    )(h, *weights, *tabs)


def _softplus(z):
    return jnp.maximum(z, 0.0) + jnp.log1p(jnp.exp(-jnp.abs(z)))


def _lru_kernel(uf_ref, ufp_ref, ufn_ref, ub_ref, ubp_ref, ubn_ref,
                cw_ref, cb_ref, wg_ref, ba_ref, bx_ref, lam_ref,
                hf_ref, hb_ref, ext_ref, a_ref, b_ref, carry_ref, *, seq_len, n_tiles):
    R = ROW_TILE
    H = SUBLANES
    j = pl.program_id(1)

    @pl.when(j == 0)
    def _():
        carry_ref[...] = jnp.zeros_like(carry_ref)

    row_in_group = lax.broadcasted_iota(jnp.int32, (R, LRU_WIDTH), 0) % H
    row = lax.broadcasted_iota(jnp.int32, (R, 1), 0)
    cw = cw_ref[...]

    for z, (cur, prev, nxt) in enumerate(((uf_ref, ufp_ref, ufn_ref), (ub_ref, ubp_ref, ubn_ref))):
        tile = j if z == 0 else n_tiles - 1 - j
        ext_ref[0:H, :] = jnp.where(tile > 0, prev[0], 0.0)
        ext_ref[H:H + R, :] = cur[0]
        ext_ref[H + R:2 * H + R, :] = jnp.where(tile < n_tiles - 1, nxt[0], 0.0)
        xc = cb_ref[...] + ext_ref[pl.ds(H - CONV_LEFT, R), :] * cw[0:1]
        for tap in range(1, CONV_WIDTH):
            xc = xc + ext_ref[pl.ds(H - CONV_LEFT + tap, R), :] * cw[tap:tap + 1]
        xb = xc.astype(BF16)
        sp = LRU_C * _softplus(-lam_ref[z:z + 1, :])
        valid = (tile * R + row) < seq_len
        for c in range(LRU_WIDTH // GATE_GROUP):
            sl = slice(c * GATE_GROUP, (c + 1) * GATE_GROUP)
            g = _dot(xb[:, sl], wg_ref[z, c])
            r = jax.nn.sigmoid(g[:, :GATE_GROUP] + ba_ref[z:z + 1, sl])
            ig = jax.nn.sigmoid(g[:, GATE_GROUP:] + bx_ref[z:z + 1, sl])
            log_a = -(r * sp[:, sl])
            a_ref[z, :, sl] = jnp.exp(log_a)
            t = jnp.tanh(log_a)
            bb = jnp.sqrt(-2.0 * t / (1.0 - t)) * (ig * xc[:, sl])
            b_ref[z, :, sl] = jnp.where(valid, bb, 0.0)
        a = a_ref[z]
        b = b_ref[z]
        for s in (1, 2, 4):
            if z == 0:
                m = row_in_group >= s
                shift = s
            else:
                m = row_in_group < H - s
                shift = R - s
            a_s = jnp.where(m, pltpu.roll(a, shift, axis=0), 1.0)
            b_s = jnp.where(m, pltpu.roll(b, shift, axis=0), 0.0)
            b = a * b_s + b
            a = a * a_s
        a_ref[z] = a
        b_ref[z] = b

    n_groups = R // H

    def step(g, carry):
        cf, cbk = carry
        rf = pl.multiple_of(g * H, H)
        hf = a_ref[0, pl.ds(rf, H), :] * cf + b_ref[0, pl.ds(rf, H), :]
        hf_ref[0, pl.ds(rf, H), :] = hf
        rb = pl.multiple_of((n_groups - 1 - g) * H, H)
        hb = a_ref[1, pl.ds(rb, H), :] * cbk + b_ref[1, pl.ds(rb, H), :]
        hb_ref[0, pl.ds(rb, H), :] = hb
        return hf[H - 1:H, :], hb[0:1, :]

    cf, cbk = lax.fori_loop(0, n_groups, step, (carry_ref[0:1, :], carry_ref[1:2, :]), unroll=4)
    carry_ref[0:1, :] = cf
    carry_ref[1:2, :] = cbk


def _lru(u, lw, seq_len):
    B, Tp, W = u.shape
    R, H = ROW_TILE, SUBLANES
    nT = Tp // R
    rpt = R // H
    last_halo = Tp // H - 1
    fwd = lambda b, j: (b, j, 0)
    bwd = lambda b, j: (b, nT - 1 - j, 0)
    fwd_prev = lambda b, j: (b, jnp.maximum(j * rpt - 1, 0), 0)
    fwd_next = lambda b, j: (b, jnp.minimum((j + 1) * rpt, last_halo), 0)
    bwd_prev = lambda b, j: (b, jnp.maximum((nT - 1 - j) * rpt - 1, 0), 0)
    bwd_next = lambda b, j: (b, jnp.minimum((nT - j) * rpt, last_halo), 0)
    tile = lambda im: pl.BlockSpec((1, R, W), im)
    halo = lambda im: pl.BlockSpec((1, H, W), im)
    weights = [lw['conv_w'], lw['conv_b'], lw['w_gate'], lw['ba'], lw['bx'], lw['lam']]
    full = lambda a: pl.BlockSpec(a.shape, lambda b, j, nd=a.ndim: (0,) * nd)
    return pl.pallas_call(
        functools.partial(_lru_kernel, seq_len=seq_len, n_tiles=nT),
        grid=(B, nT),
        in_specs=[tile(fwd), halo(fwd_prev), halo(fwd_next), tile(bwd), halo(bwd_prev), halo(bwd_next)]
                 + [full(w) for w in weights],
        out_specs=[tile(fwd), tile(bwd)],
        out_shape=[jax.ShapeDtypeStruct((B, Tp, W), F32)] * 2,
        scratch_shapes=[pltpu.VMEM((R + 2 * H, W), F32),
                        pltpu.VMEM((2, R, W), F32),
                        pltpu.VMEM((2, R, W), F32),
                        pltpu.VMEM((H, W), F32)],
        compiler_params=pltpu.CompilerParams(
            dimension_semantics=("parallel", "arbitrary"), vmem_limit_bytes=VMEM_LIMIT_BYTES),
        name="lru",
    )(u, u, u, u, u, u, *weights)


def _attn_kernel(q_ref, k_ref, v_ref, o_ref, acc_ref, *, seq_len):
    tq = q_ref.shape[2]
    n_full = seq_len // KEY_CHUNK
    rem = seq_len - n_full * KEY_CHUNK
    outs = []
    for hh in range(HEADS_PER_STEP):
        q = q_ref[0, hh]

        def chunk(start, size, m, l, hh=hh, q=q):
            kc = k_ref[0, hh, pl.ds(start, size), :]
            s = lax.dot_general(q, kc, (((1,), (1,)), ((), ())), preferred_element_type=F32)
            m_new = jnp.maximum(m, jnp.max(s, axis=-1, keepdims=True))
            alpha = jnp.exp(m - m_new)
            p = jnp.exp(s - m_new)
            l_new = alpha * l + jnp.sum(p, axis=-1, keepdims=True)
            pv = _dot(p.astype(BF16), v_ref[0, pl.ds(start, size), :])
            acc_ref[hh] = alpha * acc_ref[hh] + pv
            return m_new, l_new

        acc_ref[hh] = jnp.zeros((tq, HEAD_PAD), F32)
        m = jnp.full((tq, 1), -jnp.inf, F32)
        l = jnp.zeros((tq, 1), F32)
        m, l = lax.fori_loop(
            0, n_full,
            lambda i, c: chunk(pl.multiple_of(i * KEY_CHUNK, KEY_CHUNK), KEY_CHUNK, *c), (m, l))
        if rem:
            m, l = chunk(n_full * KEY_CHUNK, rem, m, l)
        outs.append(acc_ref[hh] / l)
    lane = lax.broadcasted_iota(jnp.int32, (tq, HEAD_PAD), 1)
    o_ref[0] = jnp.where(lane < V_HEAD, outs[0], outs[1])


def _attention(q, k, v, seq_len):
    B, NH, Tp, _ = q.shape
    assert (seq_len % KEY_CHUNK) % (2 * SUBLANES) == 0
    nq = Tp // ROW_TILE
    return pl.pallas_call(
        functools.partial(_attn_kernel, seq_len=seq_len),
        grid=(B, NH // HEADS_PER_STEP, nq),
        in_specs=[pl.BlockSpec((1, HEADS_PER_STEP, ROW_TILE, HEAD_PAD), lambda b, hp, i: (b, hp, i, 0)),
                  pl.BlockSpec((1, HEADS_PER_STEP, Tp, HEAD_PAD), lambda b, hp, i: (b, hp, 0, 0)),
                  pl.BlockSpec((1, Tp, HEAD_PAD), lambda b, hp, i: (b, 0, hp))],
        out_specs=pl.BlockSpec((1, ROW_TILE, HEAD_PAD), lambda b, hp, i: (b, i, hp)),
        out_shape=jax.ShapeDtypeStruct((B, Tp, ATTN_WIDTH), F32),
        scratch_shapes=[pltpu.VMEM((HEADS_PER_STEP, ROW_TILE, HEAD_PAD), F32)],
        compiler_params=pltpu.CompilerParams(
            dimension_semantics=("parallel", "parallel", "arbitrary"), vmem_limit_bytes=VMEM_LIMIT_BYTES),
        name="attn",
    )(q, k, v)


def _out_ffn_kernel(h_ref, hf_ref, hb_ref, gate_ref, o_ref, glru_ref, gatt_ref, wout_ref, gff_ref,
                    wup_ref, wdown_ref, out_ref, *, seq_len, ff_chunk):
    y = (hf_ref[0] + hb_ref[0]) * jax.nn.gelu(gate_ref[0])
    n_lru = _rmsnorm(y, glru_ref[...]).astype(BF16)
    n_att = _rmsnorm(o_ref[0], gatt_ref[...]).astype(BF16)
    h1 = h_ref[0] + _dot(n_lru, wout_ref[:LRU_WIDTH, :]) + _dot(n_att, wout_ref[LRU_WIDTH:, :])
    hn = _rmsnorm(h1, gff_ref[...]).astype(BF16)
    out_ref[0] = h1
    for c in range(wup_ref.shape[1] // ff_chunk):
        sl = slice(c * ff_chunk, (c + 1) * ff_chunk)
        t = jnp.maximum(_dot(hn, wup_ref[:, sl]), 0.0)
        out_ref[0] += _dot((t * t).astype(BF16), wdown_ref[sl, :])
    row = pl.program_id(1) * ROW_TILE + lax.broadcasted_iota(jnp.int32, (ROW_TILE, 1), 0)
    out_ref[0] = jnp.where(row < seq_len, out_ref[0], 0.0)


def _out_ffn(h, hf, hb, gate, o, lw, seq_len):
    B, Tp, D = h.shape
    nT = Tp // ROW_TILE
    row3 = lambda b, j: (b, j, 0)
    weights = [lw['glru'], lw['gatt'], lw['w_out'], lw['gff'], lw['w_up'], lw['w_down']]
    full = lambda a: pl.BlockSpec(a.shape, lambda b, j: (0, 0), pipeline_mode=pl.Buffered(1))
    act = lambda w: pl.BlockSpec((1, ROW_TILE, w), row3)
    return pl.pallas_call(
        functools.partial(_out_ffn_kernel, seq_len=seq_len, ff_chunk=4 * MXU_DIM),
        grid=(B, nT),
        in_specs=[act(D), act(LRU_WIDTH), act(LRU_WIDTH), act(LRU_WIDTH), act(ATTN_WIDTH)]
                 + [full(w) for w in weights],
        out_specs=act(D),
        out_shape=jax.ShapeDtypeStruct((B, Tp, D), F32),
        compiler_params=pltpu.CompilerParams(
            dimension_semantics=("parallel", "parallel"), vmem_limit_bytes=VMEM_LIMIT_BYTES),
        name="out_ffn",
    )(h, hf, hb, gate, o, *weights)


def _pad_heads(w, width):
    K = w.shape[0]
    w = w.reshape(K, MLA_HEADS, width)
    return jnp.pad(w, ((0, 0), (0, 0), (0, HEAD_PAD - width))).reshape(K, MLA_HEADS * HEAD_PAD)


def _block_diag(w):
    per = GATE_GROUP // LRU_BLOCK
    w = w.reshape(2, LRU_BLOCKS // per, per, LRU_BLOCK, LRU_BLOCK)
    eye = jnp.eye(per, dtype=w.dtype)
    return jnp.einsum('zcgij,gh->zcgihj', w, eye).reshape(2, LRU_BLOCKS // per, GATE_GROUP, GATE_GROUP)


def _layer_weights(l, norm_mix_g, w_in, conv_w, conv_b, lru_wa, lru_ba, lru_wx, lru_bx, lru_lambda,
                   q_norm_g, w_uq, kv_norm_g, w_ukv, qk_q_g, qk_k_g, out_norm_lru_g, out_norm_attn_g,
                   w_out, norm_ff_g, w_up, w_down):
    D = w_in.shape[1]
    o_kr = 2 * LRU_WIDTH + Q_LORA + KV_LORA
    w_in_p = jnp.concatenate(
        [w_in[l][:, :o_kr], jnp.zeros((D, QK_NOPE), F32), w_in[l][:, o_kr:],
         jnp.zeros((D, HEAD_PAD - QK_HEAD), F32)], axis=1)
    w_ukv_h = w_ukv[l].reshape(KV_LORA, MLA_HEADS, QK_NOPE + V_HEAD)
    pad_g = lambda g: jnp.pad(g, (0, HEAD_PAD - QK_HEAD)).reshape(1, HEAD_PAD)
    return dict(
        gmix=norm_mix_g[l].reshape(1, D), w_in=w_in_p.astype(BF16),
        gqn=q_norm_g[l].reshape(1, Q_LORA), w_uq=_pad_heads(w_uq[l], QK_HEAD).astype(BF16),
        gkvn=kv_norm_g[l].reshape(1, KV_LORA),
        w_uk=_pad_heads(w_ukv_h[:, :, :QK_NOPE].reshape(KV_LORA, -1), QK_NOPE).astype(BF16),
        w_uv=w_ukv_h[:, :, QK_NOPE:].reshape(KV_LORA, ATTN_WIDTH).astype(BF16),
        gq=pad_g(qk_q_g[l]), gk=pad_g(qk_k_g[l]),
        conv_w=conv_w[l], conv_b=conv_b[l].reshape(1, LRU_WIDTH),
        w_gate=jnp.concatenate([_block_diag(lru_wa[l]), _block_diag(lru_wx[l])], axis=-1).astype(BF16),
        ba=lru_ba[l], bx=lru_bx[l], lam=lru_lambda[l],
        glru=out_norm_lru_g[l].reshape(1, LRU_WIDTH), gatt=out_norm_attn_g[l].reshape(1, ATTN_WIDTH),
        w_out=w_out[l].astype(BF16), gff=norm_ff_g[l].reshape(1, D),
        w_up=w_up[l].astype(BF16), w_down=w_down[l].astype(BF16))


def _rope_tables(seq_len, padded_len):
    half = QK_ROPE // 2
    inv_freq = 1.0 / (ROPE_THETA ** (jnp.arange(0, QK_ROPE, 2, dtype=F32) / QK_ROPE))
    ang = jnp.arange(seq_len, dtype=F32)[:, None] * inv_freq[None, :]
    cos, sin = jnp.cos(ang), jnp.sin(ang)
    zeros = jnp.zeros((seq_len, half), F32)
    tail = jnp.zeros((seq_len, HEAD_PAD - QK_HEAD), F32)
    nope0 = jnp.zeros((seq_len, QK_NOPE), F32)
    cos_t = jnp.concatenate([jnp.ones((seq_len, QK_NOPE), F32), cos, cos, tail], axis=1)
    sin_a = jnp.concatenate([nope0, -sin, zeros, tail], axis=1)
    sin_b = jnp.concatenate([nope0, zeros, sin, tail], axis=1)
    pad = ((0, padded_len - seq_len), (0, 0))
    return [jnp.pad(t, pad) for t in (cos_t, sin_a, sin_b)]


def _run_trunk(x, meta_tokens, layers):
    B, S, D = x.shape
    T = S + N_META
    Tp = -(-T // ROW_TILE) * ROW_TILE
    meta = jnp.broadcast_to(meta_tokens.astype(x.dtype)[None], (B, N_META, D))
    h = jnp.concatenate([meta, x, jnp.zeros((B, Tp - T, D), x.dtype)], axis=1)
    tabs = _rope_tables(T, Tp)
    for lw in layers:
        u, gate, q, k, v = _in_proj(h, lw, tabs)
        hf, hb = _lru(u, lw, T)
        o = _attention(q, k, v, T)
        h = _out_ffn(h, hf, hb, gate, o, lw, T)
    return h[:, N_META:T]


def kernel(x_prompt, x_sample, meta_tokens, norm_mix_g, w_in, conv_w, conv_b, lru_wa, lru_ba, lru_wx, lru_bx,
           lru_lambda, q_norm_g, w_uq, kv_norm_g, w_ukv, qk_q_g, qk_k_g, out_norm_lru_g, out_norm_attn_g,
           w_out, norm_ff_g, w_up, w_down):
    params = (norm_mix_g, w_in, conv_w, conv_b, lru_wa, lru_ba, lru_wx, lru_bx, lru_lambda,
              q_norm_g, w_uq, kv_norm_g, w_ukv, qk_q_g, qk_k_g, out_norm_lru_g, out_norm_attn_g,
              w_out, norm_ff_g, w_up, w_down)
    layers = [_layer_weights(l, *params) for l in range(w_in.shape[0])]
    return (_run_trunk(x_prompt, meta_tokens, layers), _run_trunk(x_sample, meta_tokens, layers))
```

```python
import functools

import jax
import jax.numpy as jnp
from jax import lax
from jax.experimental import pallas as pl
from jax.experimental.pallas import tpu as pltpu

N_META = 16
LRU_WIDTH = 512
LRU_BLOCKS = 8
LRU_BLOCK = LRU_WIDTH // LRU_BLOCKS
CONV_WIDTH = 4
CONV_LEFT = 2
LRU_C = 8.0
MLA_HEADS = 8
Q_LORA = 256
KV_LORA = 128
QK_NOPE = 64
QK_ROPE = 32
QK_HEAD = QK_NOPE + QK_ROPE
V_HEAD = 64
ATTN_WIDTH = MLA_HEADS * V_HEAD
ROPE_THETA = 10000.0
EPS = 1e-6
LOG2_E = 1.4426950408889634

LANES = 128
SUBLANES = 8
MXU_DIM = 256
VMEM_LIMIT_BYTES = 56 * 1024 * 1024

HEAD_PAD = LANES
ROW_TILE = 512
KEY_CHUNK = 1024
HEADS_PER_STEP = HEAD_PAD // V_HEAD
GATE_GROUP = MXU_DIM

F32 = jnp.float32
BF16 = jnp.bfloat16


def _rmsnorm(x, g):
    return x * lax.rsqrt(jnp.mean(x * x, axis=-1, keepdims=True) + EPS) * g


def _dot(a, b):
    return jnp.dot(a, b, preferred_element_type=F32)


def _head_norm_rope(x, g, cos, sin_a, sin_b):
    ss = jnp.sum(x * x, axis=-1, keepdims=True) * (1.0 / QK_HEAD)
    y = x * lax.rsqrt(ss + EPS) * g
    hi_to_lo = pltpu.roll(y, HEAD_PAD - QK_ROPE // 2, axis=1)
    lo_to_hi = pltpu.roll(y, QK_ROPE // 2, axis=1)
    return y * cos + hi_to_lo * sin_a + lo_to_hi * sin_b


def _in_proj_kernel(h_ref, gmix_ref, win_ref, gqn_ref, wuq_ref, gkvn_ref, wuk_ref, wuv_ref,
                    gq_ref, gk_ref, cos_ref, sina_ref, sinb_ref,
                    u_ref, gate_ref, q_ref, k_ref, v_ref):
    hn = _rmsnorm(h_ref[0], gmix_ref[...])
    proj = _dot(hn.astype(BF16), win_ref[...])
    o_gate = LRU_WIDTH
    o_cq = 2 * LRU_WIDTH
    o_ckv = o_cq + Q_LORA
    o_kr = o_ckv + KV_LORA
    u_ref[0] = proj[:, :o_gate]
    gate_ref[0] = proj[:, o_gate:o_cq]
    k_rope = proj[:, o_kr:o_kr + HEAD_PAD]
    qn = _rmsnorm(proj[:, o_cq:o_ckv], gqn_ref[...]).astype(BF16)
    kvn = _rmsnorm(proj[:, o_ckv:o_kr], gkvn_ref[...]).astype(BF16)
    qf = _dot(qn, wuq_ref[...])
    kf = _dot(kvn, wuk_ref[...])
    vf = _dot(kvn, wuv_ref[...])
    value_lane = lax.broadcasted_iota(jnp.int32, (vf.shape[0], HEAD_PAD), 1) < V_HEAD
    cos, sin_a, sin_b = cos_ref[...], sina_ref[...], sinb_ref[...]
    gq, gk = gq_ref[...], gk_ref[...]
    scale = QK_HEAD ** -0.5 * LOG2_E
    for hd in range(MLA_HEADS):
        sl = slice(hd * HEAD_PAD, (hd + 1) * HEAD_PAD)
        v_ref[0, hd] = jnp.where(value_lane, vf[:, sl], 1.0).astype(BF16)
        q = _head_norm_rope(qf[:, sl], gq, cos, sin_a, sin_b)
        q_ref[0, hd] = (q * scale).astype(BF16)
        k = _head_norm_rope(kf[:, sl] + k_rope, gk, cos, sin_a, sin_b)
        k_ref[0, hd] = k.astype(BF16)


def _in_proj(h, lw, tabs):
    B, Tp, D = h.shape
    nT = Tp // ROW_TILE
    const = lambda b, j: (0, 0)
    row3 = lambda b, j: (b, j, 0)
    full = lambda a: pl.BlockSpec(a.shape, const)
    weights = [lw['gmix'], lw['w_in'], lw['gqn'], lw['w_uq'], lw['gkvn'], lw['w_uk'], lw['w_uv'],
               lw['gq'], lw['gk']]
    tab_spec = pl.BlockSpec((ROW_TILE, HEAD_PAD), lambda b, j: (j, 0))
    head_spec = pl.BlockSpec((1, MLA_HEADS, ROW_TILE, HEAD_PAD), lambda b, j: (b, 0, j, 0))
    return pl.pallas_call(
        _in_proj_kernel,
        grid=(B, nT),
        in_specs=[pl.BlockSpec((1, ROW_TILE, D), row3)] + [full(w) for w in weights] + [tab_spec] * 3,
        out_specs=[pl.BlockSpec((1, ROW_TILE, LRU_WIDTH), row3),
                   pl.BlockSpec((1, ROW_TILE, LRU_WIDTH), row3),
                   head_spec, head_spec, head_spec],
        out_shape=[jax.ShapeDtypeStruct((B, Tp, LRU_WIDTH), F32),
                   jax.ShapeDtypeStruct((B, Tp, LRU_WIDTH), F32)]
                  + [jax.ShapeDtypeStruct((B, MLA_HEADS, Tp, HEAD_PAD), BF16)] * 3,
        compiler_params=pltpu.CompilerParams(
            dimension_semantics=("parallel", "parallel"), vmem_limit_bytes=VMEM_LIMIT_BYTES),
        name="in_proj",
    )(h, *weights, *tabs)


def _softplus(z):
    return jnp.maximum(z, 0.0) + jnp.log1p(jnp.exp(-jnp.abs(z)))


def _lru_kernel(uf_ref, ufp_ref, ufn_ref, ub_ref, ubp_ref, ubn_ref,
                cw_ref, cb_ref, wg_ref, ba_ref, bx_ref, lam_ref,
                hf_ref, hb_ref, ext_ref, a_ref, b_ref, carry_ref, *, seq_len, n_tiles):
    R = ROW_TILE
    H = SUBLANES
    j = pl.program_id(1)

    @pl.when(j == 0)
    def _():
        carry_ref[...] = jnp.zeros_like(carry_ref)

    row_in_group = lax.broadcasted_iota(jnp.int32, (R, LRU_WIDTH), 0) % H
    row = lax.broadcasted_iota(jnp.int32, (R, 1), 0)
    cw = cw_ref[...]

    for z, (cur, prev, nxt) in enumerate(((uf_ref, ufp_ref, ufn_ref), (ub_ref, ubp_ref, ubn_ref))):
        tile = j if z == 0 else n_tiles - 1 - j
        ext_ref[0:H, :] = jnp.where(tile > 0, prev[0], 0.0)
        ext_ref[H:H + R, :] = cur[0]
        ext_ref[H + R:2 * H + R, :] = jnp.where(tile < n_tiles - 1, nxt[0], 0.0)
        xc = cb_ref[...] + ext_ref[pl.ds(H - CONV_LEFT, R), :] * cw[0:1]
        for tap in range(1, CONV_WIDTH):
            xc = xc + ext_ref[pl.ds(H - CONV_LEFT + tap, R), :] * cw[tap:tap + 1]
        xb = xc.astype(BF16)
        sp = LRU_C * _softplus(-lam_ref[z:z + 1, :])
        valid = (tile * R + row) < seq_len
        for c in range(LRU_WIDTH // GATE_GROUP):
            sl = slice(c * GATE_GROUP, (c + 1) * GATE_GROUP)
            g = _dot(xb[:, sl], wg_ref[z, c])
            r = jax.nn.sigmoid(g[:, :GATE_GROUP] + ba_ref[z:z + 1, sl])
            ig = jax.nn.sigmoid(g[:, GATE_GROUP:] + bx_ref[z:z + 1, sl])
            log_a = -(r * sp[:, sl])
            a_ref[z, :, sl] = jnp.exp(log_a)
            t = jnp.tanh(log_a)
            bb = jnp.sqrt(-2.0 * t / (1.0 - t)) * (ig * xc[:, sl])
            b_ref[z, :, sl] = jnp.where(valid, bb, 0.0)
        a = a_ref[z]
        b = b_ref[z]
        for s in (1, 2, 4):
            if z == 0:
                m = row_in_group >= s
                shift = s
            else:
                m = row_in_group < H - s
                shift = R - s
            a_s = jnp.where(m, pltpu.roll(a, shift, axis=0), 1.0)
            b_s = jnp.where(m, pltpu.roll(b, shift, axis=0), 0.0)
            b = a * b_s + b
            a = a * a_s
        a_ref[z] = a
        b_ref[z] = b

    n_groups = R // H

    def step(g, carry):
        cf, cbk = carry
        rf = pl.multiple_of(g * H, H)
        hf = a_ref[0, pl.ds(rf, H), :] * cf + b_ref[0, pl.ds(rf, H), :]
        hf_ref[0, pl.ds(rf, H), :] = hf
        rb = pl.multiple_of((n_groups - 1 - g) * H, H)
        hb = a_ref[1, pl.ds(rb, H), :] * cbk + b_ref[1, pl.ds(rb, H), :]
        hb_ref[0, pl.ds(rb, H), :] = hb
        return hf[H - 1:H, :], hb[0:1, :]

    cf, cbk = lax.fori_loop(0, n_groups, step, (carry_ref[0:1, :], carry_ref[1:2, :]), unroll=4)
    carry_ref[0:1, :] = cf
    carry_ref[1:2, :] = cbk


def _lru(u, lw, seq_len):
    B, Tp, W = u.shape
    R, H = ROW_TILE, SUBLANES
    nT = Tp // R
    rpt = R // H
    last_halo = Tp // H - 1
    fwd = lambda b, j: (b, j, 0)
    bwd = lambda b, j: (b, nT - 1 - j, 0)
    fwd_prev = lambda b, j: (b, jnp.maximum(j * rpt - 1, 0), 0)
    fwd_next = lambda b, j: (b, jnp.minimum((j + 1) * rpt, last_halo), 0)
    bwd_prev = lambda b, j: (b, jnp.maximum((nT - 1 - j) * rpt - 1, 0), 0)
    bwd_next = lambda b, j: (b, jnp.minimum((nT - j) * rpt, last_halo), 0)
    tile = lambda im: pl.BlockSpec((1, R, W), im)
    halo = lambda im: pl.BlockSpec((1, H, W), im)
    weights = [lw['conv_w'], lw['conv_b'], lw['w_gate'], lw['ba'], lw['bx'], lw['lam']]
    full = lambda a: pl.BlockSpec(a.shape, lambda b, j, nd=a.ndim: (0,) * nd)
    return pl.pallas_call(
        functools.partial(_lru_kernel, seq_len=seq_len, n_tiles=nT),
        grid=(B, nT),
        in_specs=[tile(fwd), halo(fwd_prev), halo(fwd_next), tile(bwd), halo(bwd_prev), halo(bwd_next)]
                 + [full(w) for w in weights],
        out_specs=[tile(fwd), tile(bwd)],
        out_shape=[jax.ShapeDtypeStruct((B, Tp, W), F32)] * 2,
        scratch_shapes=[pltpu.VMEM((R + 2 * H, W), F32),
                        pltpu.VMEM((2, R, W), F32),
                        pltpu.VMEM((2, R, W), F32),
                        pltpu.VMEM((H, W), F32)],
        compiler_params=pltpu.CompilerParams(
            dimension_semantics=("parallel", "arbitrary"), vmem_limit_bytes=VMEM_LIMIT_BYTES),
        name="lru",
    )(u, u, u, u, u, u, *weights)


def _attn_kernel(q_ref, k_ref, v_ref, o_ref, acc_ref, *, seq_len):
    tq = q_ref.shape[2]
    n_full = seq_len // KEY_CHUNK
    rem = seq_len - n_full * KEY_CHUNK
    heads = range(HEADS_PER_STEP)
    qs = [q_ref[0, hh] for hh in heads]

    def chunk(start, size, ms):
        new_ms = []
        for hh in heads:
            kc = k_ref[0, hh, pl.ds(start, size), :]
            s = lax.dot_general(qs[hh], kc, (((1,), (1,)), ((), ())), preferred_element_type=F32)
            m_new = jnp.maximum(ms[hh], jnp.max(s, axis=-1, keepdims=True))
            alpha = jnp.exp2(ms[hh] - m_new)
            p = jnp.exp2(s - m_new).astype(BF16)
            acc_ref[hh] = alpha * acc_ref[hh] + _dot(p, v_ref[0, hh, pl.ds(start, size), :])
            new_ms.append(m_new)
        return tuple(new_ms)

    for hh in heads:
        acc_ref[hh] = jnp.zeros((tq, HEAD_PAD), F32)
    ms = tuple(jnp.full((tq, 1), -jnp.inf, F32) for _ in heads)
    ms = lax.fori_loop(
        0, n_full, lambda i, c: chunk(pl.multiple_of(i * KEY_CHUNK, KEY_CHUNK), KEY_CHUNK, c), ms)
    if rem:
        chunk(n_full * KEY_CHUNK, rem, ms)
    outs = [acc_ref[hh] / acc_ref[hh][:, V_HEAD:V_HEAD + 1] for hh in heads]
    lane = lax.broadcasted_iota(jnp.int32, (tq, HEAD_PAD), 1)
    o_ref[0] = jnp.where(lane < V_HEAD, outs[0], pltpu.roll(outs[1], V_HEAD, axis=1))


def _attention(q, k, v, seq_len):
    B, NH, Tp, _ = q.shape
    assert (seq_len % KEY_CHUNK) % (2 * SUBLANES) == 0
    nq = Tp // ROW_TILE
    return pl.pallas_call(
        functools.partial(_attn_kernel, seq_len=seq_len),
        grid=(B, NH // HEADS_PER_STEP, nq),
        in_specs=[pl.BlockSpec((1, HEADS_PER_STEP, ROW_TILE, HEAD_PAD), lambda b, hp, i: (b, hp, i, 0)),
                  pl.BlockSpec((1, HEADS_PER_STEP, Tp, HEAD_PAD), lambda b, hp, i: (b, hp, 0, 0)),
                  pl.BlockSpec((1, HEADS_PER_STEP, Tp, HEAD_PAD), lambda b, hp, i: (b, hp, 0, 0))],
        out_specs=pl.BlockSpec((1, ROW_TILE, HEAD_PAD), lambda b, hp, i: (b, i, hp)),
        out_shape=jax.ShapeDtypeStruct((B, Tp, ATTN_WIDTH), F32),
        scratch_shapes=[pltpu.VMEM((HEADS_PER_STEP, ROW_TILE, HEAD_PAD), F32)],
        compiler_params=pltpu.CompilerParams(
            dimension_semantics=("parallel", "parallel", "arbitrary"), vmem_limit_bytes=VMEM_LIMIT_BYTES),
        name="attn",
    )(q, k, v)


def _out_ffn_kernel(h_ref, hf_ref, hb_ref, gate_ref, o_ref, glru_ref, gatt_ref, wout_ref, gff_ref,
                    wup_ref, wdown_ref, out_ref, *, seq_len, ff_chunk):
    y = (hf_ref[0] + hb_ref[0]) * jax.nn.gelu(gate_ref[0])
    n_lru = _rmsnorm(y, glru_ref[...]).astype(BF16)
    n_att = _rmsnorm(o_ref[0], gatt_ref[...]).astype(BF16)
    h1 = h_ref[0] + _dot(n_lru, wout_ref[:LRU_WIDTH, :]) + _dot(n_att, wout_ref[LRU_WIDTH:, :])
    hn = _rmsnorm(h1, gff_ref[...]).astype(BF16)
    out_ref[0] = h1
    for c in range(wup_ref.shape[1] // ff_chunk):
        sl = slice(c * ff_chunk, (c + 1) * ff_chunk)
        t = jnp.maximum(_dot(hn, wup_ref[:, sl]), 0.0)
        out_ref[0] += _dot((t * t).astype(BF16), wdown_ref[sl, :])
    row = pl.program_id(1) * ROW_TILE + lax.broadcasted_iota(jnp.int32, (ROW_TILE, 1), 0)
    out_ref[0] = jnp.where(row < seq_len, out_ref[0], 0.0)


def _out_ffn(h, hf, hb, gate, o, lw, seq_len):
    B, Tp, D = h.shape
    nT = Tp // ROW_TILE
    row3 = lambda b, j: (b, j, 0)
    weights = [lw['glru'], lw['gatt'], lw['w_out'], lw['gff'], lw['w_up'], lw['w_down']]
    full = lambda a: pl.BlockSpec(a.shape, lambda b, j: (0, 0), pipeline_mode=pl.Buffered(1))
    act = lambda w: pl.BlockSpec((1, ROW_TILE, w), row3)
    return pl.pallas_call(
        functools.partial(_out_ffn_kernel, seq_len=seq_len, ff_chunk=4 * MXU_DIM),
        grid=(B, nT),
        in_specs=[act(D), act(LRU_WIDTH), act(LRU_WIDTH), act(LRU_WIDTH), act(ATTN_WIDTH)]
                 + [full(w) for w in weights],
        out_specs=act(D),
        out_shape=jax.ShapeDtypeStruct((B, Tp, D), F32),
        compiler_params=pltpu.CompilerParams(
            dimension_semantics=("parallel", "parallel"), vmem_limit_bytes=VMEM_LIMIT_BYTES),
        name="out_ffn",
    )(h, hf, hb, gate, o, *weights)


def _pad_heads(w, width):
    K = w.shape[0]
    w = w.reshape(K, MLA_HEADS, width)
    return jnp.pad(w, ((0, 0), (0, 0), (0, HEAD_PAD - width))).reshape(K, MLA_HEADS * HEAD_PAD)


def _block_diag(w):
    per = GATE_GROUP // LRU_BLOCK
    w = w.reshape(2, LRU_BLOCKS // per, per, LRU_BLOCK, LRU_BLOCK)
    eye = jnp.eye(per, dtype=w.dtype)
    return jnp.einsum('zcgij,gh->zcgihj', w, eye).reshape(2, LRU_BLOCKS // per, GATE_GROUP, GATE_GROUP)


def _layer_weights(l, norm_mix_g, w_in, conv_w, conv_b, lru_wa, lru_ba, lru_wx, lru_bx, lru_lambda,
                   q_norm_g, w_uq, kv_norm_g, w_ukv, qk_q_g, qk_k_g, out_norm_lru_g, out_norm_attn_g,
                   w_out, norm_ff_g, w_up, w_down):
    D = w_in.shape[1]
    o_kr = 2 * LRU_WIDTH + Q_LORA + KV_LORA
    w_in_p = jnp.concatenate(
        [w_in[l][:, :o_kr], jnp.zeros((D, QK_NOPE), F32), w_in[l][:, o_kr:],
         jnp.zeros((D, HEAD_PAD - QK_HEAD), F32)], axis=1)
    w_ukv_h = w_ukv[l].reshape(KV_LORA, MLA_HEADS, QK_NOPE + V_HEAD)
    pad_g = lambda g: jnp.pad(g, (0, HEAD_PAD - QK_HEAD)).reshape(1, HEAD_PAD)
    return dict(
        gmix=norm_mix_g[l].reshape(1, D), w_in=w_in_p.astype(BF16),
        gqn=q_norm_g[l].reshape(1, Q_LORA), w_uq=_pad_heads(w_uq[l], QK_HEAD).astype(BF16),
        gkvn=kv_norm_g[l].reshape(1, KV_LORA),
        w_uk=_pad_heads(w_ukv_h[:, :, :QK_NOPE].reshape(KV_LORA, -1), QK_NOPE).astype(BF16),
        w_uv=_pad_heads(w_ukv_h[:, :, QK_NOPE:].reshape(KV_LORA, -1), V_HEAD).astype(BF16),
        gq=pad_g(qk_q_g[l]), gk=pad_g(qk_k_g[l]),
        conv_w=conv_w[l], conv_b=conv_b[l].reshape(1, LRU_WIDTH),
        w_gate=jnp.concatenate([_block_diag(lru_wa[l]), _block_diag(lru_wx[l])], axis=-1).astype(BF16),
        ba=lru_ba[l], bx=lru_bx[l], lam=lru_lambda[l],
        glru=out_norm_lru_g[l].reshape(1, LRU_WIDTH), gatt=out_norm_attn_g[l].reshape(1, ATTN_WIDTH),
        w_out=w_out[l].astype(BF16), gff=norm_ff_g[l].reshape(1, D),
        w_up=w_up[l].astype(BF16), w_down=w_down[l].astype(BF16))


def _rope_tables(seq_len, padded_len):
    half = QK_ROPE // 2
    inv_freq = 1.0 / (ROPE_THETA ** (jnp.arange(0, QK_ROPE, 2, dtype=F32) / QK_ROPE))
    ang = jnp.arange(seq_len, dtype=F32)[:, None] * inv_freq[None, :]
    cos, sin = jnp.cos(ang), jnp.sin(ang)
    zeros = jnp.zeros((seq_len, half), F32)
    tail = jnp.zeros((seq_len, HEAD_PAD - QK_HEAD), F32)
    nope0 = jnp.zeros((seq_len, QK_NOPE), F32)
    cos_t = jnp.concatenate([jnp.ones((seq_len, QK_NOPE), F32), cos, cos, tail], axis=1)
    sin_a = jnp.concatenate([nope0, -sin, zeros, tail], axis=1)
    sin_b = jnp.concatenate([nope0, zeros, sin, tail], axis=1)
    pad = ((0, padded_len - seq_len), (0, 0))
    return [jnp.pad(t, pad) for t in (cos_t, sin_a, sin_b)]


def _run_trunk(x, meta_tokens, layers):
    B, S, D = x.shape
    T = S + N_META
    Tp = -(-T // ROW_TILE) * ROW_TILE
    meta = jnp.broadcast_to(meta_tokens.astype(x.dtype)[None], (B, N_META, D))
    h = jnp.concatenate([meta, x, jnp.zeros((B, Tp - T, D), x.dtype)], axis=1)
    tabs = _rope_tables(T, Tp)
    for lw in layers:
        u, gate, q, k, v = _in_proj(h, lw, tabs)
        hf, hb = _lru(u, lw, T)
        o = _attention(q, k, v, T)
        h = _out_ffn(h, hf, hb, gate, o, lw, T)
    return h[:, N_META:T]


def kernel(x_prompt, x_sample, meta_tokens, norm_mix_g, w_in, conv_w, conv_b, lru_wa, lru_ba, lru_wx, lru_bx,
           lru_lambda, q_norm_g, w_uq, kv_norm_g, w_ukv, qk_q_g, qk_k_g, out_norm_lru_g, out_norm_attn_g,
           w_out, norm_ff_g, w_up, w_down):
    params = (norm_mix_g, w_in, conv_w, conv_b, lru_wa, lru_ba, lru_wx, lru_bx, lru_lambda,
              q_norm_g, w_uq, kv_norm_g, w_ukv, qk_q_g, qk_k_g, out_norm_lru_g, out_norm_attn_g,
              w_out, norm_ff_g, w_up, w_down)
    layers = [_layer_weights(l, *params) for l in range(w_in.shape[0])]
    return (_run_trunk(x_prompt, meta_tokens, layers), _run_trunk(x_sample, meta_tokens, layers))
```

```python
import functools

import jax
import jax.numpy as jnp
from jax import lax
from jax.experimental import pallas as pl
from jax.experimental.pallas import tpu as pltpu

N_META = 16
LRU_WIDTH = 512
LRU_BLOCKS = 8
LRU_BLOCK = LRU_WIDTH // LRU_BLOCKS
CONV_WIDTH = 4
CONV_LEFT = 2
LRU_C = 8.0
MLA_HEADS = 8
Q_LORA = 256
KV_LORA = 128
QK_NOPE = 64
QK_ROPE = 32
QK_HEAD = QK_NOPE + QK_ROPE
V_HEAD = 64
ATTN_WIDTH = MLA_HEADS * V_HEAD
ROPE_THETA = 10000.0
EPS = 1e-6
LOG2_E = 1.4426950408889634

LANES = 128
SUBLANES = 8
MXU_DIM = 256
VMEM_LIMIT_BYTES = 56 * 1024 * 1024

HEAD_PAD = LANES
ROW_TILE = 512
KEY_CHUNK = ROW_TILE
SCORE_RING = 4
SCORE_LOOKAHEAD = 2
HEADS_PER_STEP = HEAD_PAD // V_HEAD
V_ROWS = V_HEAD + 2 * SUBLANES
GATE_GROUP = MXU_DIM

F32 = jnp.float32
BF16 = jnp.bfloat16


def _rmsnorm(x, g):
    return x * lax.rsqrt(jnp.mean(x * x, axis=-1, keepdims=True) + EPS) * g


def _dot(a, b):
    return jnp.dot(a, b, preferred_element_type=F32)


def _head_norm_rope(x, g, cos, sin_a, sin_b):
    ss = jnp.sum(x * x, axis=-1, keepdims=True) * (1.0 / QK_HEAD)
    y = x * lax.rsqrt(ss + EPS) * g
    hi_to_lo = pltpu.roll(y, HEAD_PAD - QK_ROPE // 2, axis=1)
    lo_to_hi = pltpu.roll(y, QK_ROPE // 2, axis=1)
    return y * cos + hi_to_lo * sin_a + lo_to_hi * sin_b


def _in_proj_kernel(h_ref, gmix_ref, win_ref, gqn_ref, wuq_ref, gkvn_ref, wuk_ref, wuv_ref,
                    gq_ref, gk_ref, cos_ref, sina_ref, sinb_ref,
                    u_ref, gate_ref, q_ref, k_ref, v_ref):
    hn = _rmsnorm(h_ref[0], gmix_ref[...])
    proj = _dot(hn.astype(BF16), win_ref[...])
    o_gate = LRU_WIDTH
    o_cq = 2 * LRU_WIDTH
    o_ckv = o_cq + Q_LORA
    o_kr = o_ckv + KV_LORA
    u_ref[0] = proj[:, :o_gate]
    gate_ref[0] = proj[:, o_gate:o_cq]
    k_rope = proj[:, o_kr:o_kr + HEAD_PAD]
    qn = _rmsnorm(proj[:, o_cq:o_ckv], gqn_ref[...]).astype(BF16)
    kvn_f = _rmsnorm(proj[:, o_ckv:o_kr], gkvn_ref[...])
    kvn = kvn_f.astype(BF16)
    qf = _dot(qn, wuq_ref[...])
    kf = _dot(kvn, wuk_ref[...])
    vt = _dot(wuv_ref[...], kvn_f.T.astype(BF16))
    value_row = lax.broadcasted_iota(jnp.int32, (V_ROWS, vt.shape[1]), 0) < V_HEAD
    cos, sin_a, sin_b = cos_ref[...], sina_ref[...], sinb_ref[...]
    gq, gk = gq_ref[...], gk_ref[...]
    scale = QK_HEAD ** -0.5 * LOG2_E
    for hd in range(MLA_HEADS):
        sl = slice(hd * HEAD_PAD, (hd + 1) * HEAD_PAD)
        v_ref[0, hd, 0] = jnp.where(value_row, vt[hd * V_ROWS:(hd + 1) * V_ROWS], 1.0).astype(BF16)
        q = _head_norm_rope(qf[:, sl], gq, cos, sin_a, sin_b)
        q_ref[0, hd] = (q * scale).astype(BF16)
        k = _head_norm_rope(kf[:, sl] + k_rope, gk, cos, sin_a, sin_b)
        k_ref[0, hd] = k.astype(BF16)


def _in_proj(h, lw, tabs):
    B, Tp, D = h.shape
    nT = Tp // ROW_TILE
    const = lambda b, j: (0, 0)
    row3 = lambda b, j: (b, j, 0)
    full = lambda a: pl.BlockSpec(a.shape, const)
    weights = [lw['gmix'], lw['w_in'], lw['gqn'], lw['w_uq'], lw['gkvn'], lw['w_uk'], lw['w_uv'],
               lw['gq'], lw['gk']]
    tab_spec = pl.BlockSpec((ROW_TILE, HEAD_PAD), lambda b, j: (j, 0))
    head_spec = pl.BlockSpec((1, MLA_HEADS, ROW_TILE, HEAD_PAD), lambda b, j: (b, 0, j, 0))
    return pl.pallas_call(
        _in_proj_kernel,
        grid=(B, nT),
        in_specs=[pl.BlockSpec((1, ROW_TILE, D), row3)] + [full(w) for w in weights] + [tab_spec] * 3,
        out_specs=[pl.BlockSpec((1, ROW_TILE, LRU_WIDTH), row3),
                   pl.BlockSpec((1, ROW_TILE, LRU_WIDTH), row3),
                   head_spec, head_spec,
                   pl.BlockSpec((1, MLA_HEADS, 1, V_ROWS, ROW_TILE), lambda b, j: (b, 0, j, 0, 0))],
        out_shape=[jax.ShapeDtypeStruct((B, Tp, LRU_WIDTH), F32),
                   jax.ShapeDtypeStruct((B, Tp, LRU_WIDTH), F32),
                   jax.ShapeDtypeStruct((B, MLA_HEADS, Tp, HEAD_PAD), BF16),
                   jax.ShapeDtypeStruct((B, MLA_HEADS, Tp, HEAD_PAD), BF16),
                   jax.ShapeDtypeStruct((B, MLA_HEADS, nT, V_ROWS, ROW_TILE), BF16)],
        compiler_params=pltpu.CompilerParams(
            dimension_semantics=("parallel", "parallel"), vmem_limit_bytes=VMEM_LIMIT_BYTES),
        name="in_proj",
    )(h, *weights, *tabs)


def _softplus(z):
    return jnp.maximum(z, 0.0) + jnp.log1p(jnp.exp(-jnp.abs(z)))


def _lru_kernel(uf_ref, ufp_ref, ufn_ref, ub_ref, ubp_ref, ubn_ref,
                cw_ref, cb_ref, wg_ref, ba_ref, bx_ref, lam_ref,
                hf_ref, hb_ref, ext_ref, a_ref, b_ref, carry_ref, *, seq_len, n_tiles):
    R = ROW_TILE
    H = SUBLANES
    j = pl.program_id(1)

    @pl.when(j == 0)
    def _():
        carry_ref[...] = jnp.zeros_like(carry_ref)

    row_in_group = lax.broadcasted_iota(jnp.int32, (R, LRU_WIDTH), 0) % H
    row = lax.broadcasted_iota(jnp.int32, (R, 1), 0)
    cw = cw_ref[...]

    for z, (cur, prev, nxt) in enumerate(((uf_ref, ufp_ref, ufn_ref), (ub_ref, ubp_ref, ubn_ref))):
        tile = j if z == 0 else n_tiles - 1 - j
        ext_ref[0:H, :] = jnp.where(tile > 0, prev[0], 0.0)
        ext_ref[H:H + R, :] = cur[0]
        ext_ref[H + R:2 * H + R, :] = jnp.where(tile < n_tiles - 1, nxt[0], 0.0)
        xc = cb_ref[...] + ext_ref[pl.ds(H - CONV_LEFT, R), :] * cw[0:1]
        for tap in range(1, CONV_WIDTH):
            xc = xc + ext_ref[pl.ds(H - CONV_LEFT + tap, R), :] * cw[tap:tap + 1]
        xb = xc.astype(BF16)
        sp = LRU_C * _softplus(-lam_ref[z:z + 1, :])
        valid = (tile * R + row) < seq_len
        for c in range(LRU_WIDTH // GATE_GROUP):
            sl = slice(c * GATE_GROUP, (c + 1) * GATE_GROUP)
            g = _dot(xb[:, sl], wg_ref[z, c])
            r = jax.nn.sigmoid(g[:, :GATE_GROUP] + ba_ref[z:z + 1, sl])
            ig = jax.nn.sigmoid(g[:, GATE_GROUP:] + bx_ref[z:z + 1, sl])
            log_a = -(r * sp[:, sl])
            a_ref[z, :, sl] = jnp.exp(log_a)
            t = jnp.tanh(log_a)
            bb = jnp.sqrt(-2.0 * t / (1.0 - t)) * (ig * xc[:, sl])
            b_ref[z, :, sl] = jnp.where(valid, bb, 0.0)
        a = a_ref[z]
        b = b_ref[z]
        for s in (1, 2, 4):
            if z == 0:
                m = row_in_group >= s
                shift = s
            else:
                m = row_in_group < H - s
                shift = R - s
            a_s = jnp.where(m, pltpu.roll(a, shift, axis=0), 1.0)
            b_s = jnp.where(m, pltpu.roll(b, shift, axis=0), 0.0)
            b = a * b_s + b
            a = a * a_s
        a_ref[z] = a
        b_ref[z] = b

    n_groups = R // H

    def step(g, carry):
        cf, cbk = carry
        rf = pl.multiple_of(g * H, H)
        hf = a_ref[0, pl.ds(rf, H), :] * cf + b_ref[0, pl.ds(rf, H), :]
        hf_ref[0, pl.ds(rf, H), :] = hf
        rb = pl.multiple_of((n_groups - 1 - g) * H, H)
        hb = a_ref[1, pl.ds(rb, H), :] * cbk + b_ref[1, pl.ds(rb, H), :]
        hb_ref[0, pl.ds(rb, H), :] = hb
        return hf[H - 1:H, :], hb[0:1, :]

    cf, cbk = lax.fori_loop(0, n_groups, step, (carry_ref[0:1, :], carry_ref[1:2, :]), unroll=4)
    carry_ref[0:1, :] = cf
    carry_ref[1:2, :] = cbk


def _lru(u, lw, seq_len):
    B, Tp, W = u.shape
    R, H = ROW_TILE, SUBLANES
    nT = Tp // R
    rpt = R // H
    last_halo = Tp // H - 1
    fwd = lambda b, j: (b, j, 0)
    bwd = lambda b, j: (b, nT - 1 - j, 0)
    fwd_prev = lambda b, j: (b, jnp.maximum(j * rpt - 1, 0), 0)
    fwd_next = lambda b, j: (b, jnp.minimum((j + 1) * rpt, last_halo), 0)
    bwd_prev = lambda b, j: (b, jnp.maximum((nT - 1 - j) * rpt - 1, 0), 0)
    bwd_next = lambda b, j: (b, jnp.minimum((nT - j) * rpt, last_halo), 0)
    tile = lambda im: pl.BlockSpec((1, R, W), im)
    halo = lambda im: pl.BlockSpec((1, H, W), im)
    weights = [lw['conv_w'], lw['conv_b'], lw['w_gate'], lw['ba'], lw['bx'], lw['lam']]
    full = lambda a: pl.BlockSpec(a.shape, lambda b, j, nd=a.ndim: (0,) * nd)
    return pl.pallas_call(
        functools.partial(_lru_kernel, seq_len=seq_len, n_tiles=nT),
        grid=(B, nT),
        in_specs=[tile(fwd), halo(fwd_prev), halo(fwd_next), tile(bwd), halo(bwd_prev), halo(bwd_next)]
                 + [full(w) for w in weights],
        out_specs=[tile(fwd), tile(bwd)],
        out_shape=[jax.ShapeDtypeStruct((B, Tp, W), F32)] * 2,
        scratch_shapes=[pltpu.VMEM((R + 2 * H, W), F32),
                        pltpu.VMEM((2, R, W), F32),
                        pltpu.VMEM((2, R, W), F32),
                        pltpu.VMEM((H, W), F32)],
        compiler_params=pltpu.CompilerParams(
            dimension_semantics=("parallel", "arbitrary"), vmem_limit_bytes=VMEM_LIMIT_BYTES),
        name="lru",
    )(u, u, u, u, u, u, *weights)


def _attn_kernel(q_ref, k_ref, vt_ref, o_ref, acc_ref, st_ref, *, seq_len):
    tq = q_ref.shape[2]
    n_full = seq_len // KEY_CHUNK
    rem = seq_len - n_full * KEY_CHUNK
    heads = range(HEADS_PER_STEP)
    qs = [q_ref[0, hh] for hh in heads]

    def scores(hh, start, size):
        kc = k_ref[0, hh, pl.ds(start, size), :]
        return lax.dot_general(kc, qs[hh], (((1,), (1,)), ((), ())), preferred_element_type=F32)

    def produce(slot, hh, i):
        st = scores(hh, pl.multiple_of(i * KEY_CHUNK, KEY_CHUNK), KEY_CHUNK)
        st_ref[slot, hh] = st
        return jnp.max(st, axis=0, keepdims=True)

    def consume(hh, st, chunk_max, vt, m):
        m_new = jnp.maximum(m, chunk_max)
        alpha = jnp.exp2(m - m_new)
        pt = jnp.exp2(st - m_new).astype(BF16)
        acc_ref[hh] = alpha * acc_ref[hh] + _dot(vt, pt)
        return m_new

    def ring_pass(j, carry, n_produce=SCORE_RING):
        ms, pending = carry
        ms = list(ms)
        pending = [list(p) for p in pending]
        for c in range(SCORE_RING):
            i = SCORE_RING * j + c
            chunk_max = pending.pop(0)
            ahead = []
            for hh in heads:
                if c < n_produce:
                    ahead.append(produce((c + SCORE_LOOKAHEAD) % SCORE_RING, hh, i + SCORE_LOOKAHEAD))
                ms[hh] = consume(hh, st_ref[c, hh], chunk_max[hh], vt_ref[0, hh, i], ms[hh])
            if ahead:
                pending.append(ahead)
        return tuple(ms), tuple(tuple(p) for p in pending)

    for hh in heads:
        acc_ref[hh] = jnp.zeros((V_ROWS, tq), F32)
    ms = tuple(jnp.full((1, tq), -jnp.inf, F32) for _ in heads)
    if n_full:
        n_pass = n_full // SCORE_RING
        carry = (ms, tuple(tuple(produce(c, hh, c) for hh in heads) for c in range(SCORE_LOOKAHEAD)))
        carry = lax.fori_loop(0, n_pass - 1, ring_pass, carry)
        ms, _ = ring_pass(n_pass - 1, carry, n_produce=SCORE_RING - SCORE_LOOKAHEAD)
    if rem:
        blk, off = divmod(n_full * KEY_CHUNK, ROW_TILE)
        assert off + rem <= ROW_TILE
        for hh in heads:
            st = scores(hh, n_full * KEY_CHUNK, rem)
            consume(hh, st, jnp.max(st, axis=0, keepdims=True), vt_ref[0, hh, blk, :, off:off + rem], ms[hh])
    outs = [acc_ref[hh, 0:V_HEAD, :] / acc_ref[hh, V_HEAD:V_HEAD + 1, :] for hh in heads]
    o_ref[0] = jnp.concatenate(outs, axis=0).T


def _attention(q, k, vt, seq_len):
    B, NH, Tp, _ = q.shape
    assert (seq_len % KEY_CHUNK) % (2 * SUBLANES) == 0 and KEY_CHUNK == ROW_TILE
    assert (seq_len // KEY_CHUNK) % SCORE_RING == 0
    nq = Tp // ROW_TILE
    return pl.pallas_call(
        functools.partial(_attn_kernel, seq_len=seq_len),
        grid=(B, NH // HEADS_PER_STEP, nq),
        in_specs=[pl.BlockSpec((1, HEADS_PER_STEP, ROW_TILE, HEAD_PAD), lambda b, hp, i: (b, hp, i, 0)),
                  pl.BlockSpec((1, HEADS_PER_STEP, Tp, HEAD_PAD), lambda b, hp, i: (b, hp, 0, 0)),
                  pl.BlockSpec((1, HEADS_PER_STEP, nq, V_ROWS, ROW_TILE), lambda b, hp, i: (b, hp, 0, 0, 0))],
        out_specs=pl.BlockSpec((1, ROW_TILE, HEAD_PAD), lambda b, hp, i: (b, i, hp)),
        out_shape=jax.ShapeDtypeStruct((B, Tp, ATTN_WIDTH), F32),
        scratch_shapes=[pltpu.VMEM((HEADS_PER_STEP, V_ROWS, ROW_TILE), F32),
                        pltpu.VMEM((SCORE_RING, HEADS_PER_STEP, KEY_CHUNK, ROW_TILE), F32)],
        compiler_params=pltpu.CompilerParams(
            dimension_semantics=("parallel", "parallel", "arbitrary"), vmem_limit_bytes=VMEM_LIMIT_BYTES),
        name="attn",
    )(q, k, vt)


def _out_ffn_kernel(h_ref, hf_ref, hb_ref, gate_ref, o_ref, glru_ref, gatt_ref, wout_ref, gff_ref,
                    wup_ref, wdown_ref, out_ref, *, seq_len, ff_chunk):
    y = (hf_ref[0] + hb_ref[0]) * jax.nn.gelu(gate_ref[0])
    n_lru = _rmsnorm(y, glru_ref[...]).astype(BF16)
    n_att = _rmsnorm(o_ref[0], gatt_ref[...]).astype(BF16)
    h1 = h_ref[0] + _dot(n_lru, wout_ref[:LRU_WIDTH, :]) + _dot(n_att, wout_ref[LRU_WIDTH:, :])
    hn = _rmsnorm(h1, gff_ref[...]).astype(BF16)
    out_ref[0] = h1
    for c in range(wup_ref.shape[1] // ff_chunk):
        sl = slice(c * ff_chunk, (c + 1) * ff_chunk)
        t = jnp.maximum(_dot(hn, wup_ref[:, sl]), 0.0)
        out_ref[0] += _dot((t * t).astype(BF16), wdown_ref[sl, :])
    row = pl.program_id(1) * ROW_TILE + lax.broadcasted_iota(jnp.int32, (ROW_TILE, 1), 0)
    out_ref[0] = jnp.where(row < seq_len, out_ref[0], 0.0)


def _out_ffn(h, hf, hb, gate, o, lw, seq_len):
    B, Tp, D = h.shape
    nT = Tp // ROW_TILE
    row3 = lambda b, j: (b, j, 0)
    weights = [lw['glru'], lw['gatt'], lw['w_out'], lw['gff'], lw['w_up'], lw['w_down']]
    full = lambda a: pl.BlockSpec(a.shape, lambda b, j: (0, 0), pipeline_mode=pl.Buffered(1))
    act = lambda w: pl.BlockSpec((1, ROW_TILE, w), row3)
    return pl.pallas_call(
        functools.partial(_out_ffn_kernel, seq_len=seq_len, ff_chunk=4 * MXU_DIM),
        grid=(B, nT),
        in_specs=[act(D), act(LRU_WIDTH), act(LRU_WIDTH), act(LRU_WIDTH), act(ATTN_WIDTH)]
                 + [full(w) for w in weights],
        out_specs=act(D),
        out_shape=jax.ShapeDtypeStruct((B, Tp, D), F32),
        compiler_params=pltpu.CompilerParams(
            dimension_semantics=("parallel", "parallel"), vmem_limit_bytes=VMEM_LIMIT_BYTES),
        name="out_ffn",
    )(h, hf, hb, gate, o, *weights)


def _pad_heads(w, width):
    K = w.shape[0]
    w = w.reshape(K, MLA_HEADS, width)
    return jnp.pad(w, ((0, 0), (0, 0), (0, HEAD_PAD - width))).reshape(K, MLA_HEADS * HEAD_PAD)


def _block_diag(w):
    per = GATE_GROUP // LRU_BLOCK
    w = w.reshape(2, LRU_BLOCKS // per, per, LRU_BLOCK, LRU_BLOCK)
    eye = jnp.eye(per, dtype=w.dtype)
    return jnp.einsum('zcgij,gh->zcgihj', w, eye).reshape(2, LRU_BLOCKS // per, GATE_GROUP, GATE_GROUP)


def _layer_weights(l, norm_mix_g, w_in, conv_w, conv_b, lru_wa, lru_ba, lru_wx, lru_bx, lru_lambda,
                   q_norm_g, w_uq, kv_norm_g, w_ukv, qk_q_g, qk_k_g, out_norm_lru_g, out_norm_attn_g,
                   w_out, norm_ff_g, w_up, w_down):
    D = w_in.shape[1]
    o_kr = 2 * LRU_WIDTH + Q_LORA + KV_LORA
    w_in_p = jnp.concatenate(
        [w_in[l][:, :o_kr], jnp.zeros((D, QK_NOPE), F32), w_in[l][:, o_kr:],
         jnp.zeros((D, HEAD_PAD - QK_HEAD), F32)], axis=1)
    w_ukv_h = w_ukv[l].reshape(KV_LORA, MLA_HEADS, QK_NOPE + V_HEAD)
    pad_g = lambda g: jnp.pad(g, (0, HEAD_PAD - QK_HEAD)).reshape(1, HEAD_PAD)
    return dict(
        gmix=norm_mix_g[l].reshape(1, D), w_in=w_in_p.astype(BF16),
        gqn=q_norm_g[l].reshape(1, Q_LORA), w_uq=_pad_heads(w_uq[l], QK_HEAD).astype(BF16),
        gkvn=kv_norm_g[l].reshape(1, KV_LORA),
        w_uk=_pad_heads(w_ukv_h[:, :, :QK_NOPE].reshape(KV_LORA, -1), QK_NOPE).astype(BF16),
        w_uv=jnp.pad(w_ukv_h[:, :, QK_NOPE:].transpose(1, 2, 0), ((0, 0), (0, V_ROWS - V_HEAD), (0, 0)))
        .reshape(MLA_HEADS * V_ROWS, KV_LORA).astype(BF16),
        gq=pad_g(qk_q_g[l]), gk=pad_g(qk_k_g[l]),
        conv_w=conv_w[l], conv_b=conv_b[l].reshape(1, LRU_WIDTH),
        w_gate=jnp.concatenate([_block_diag(lru_wa[l]), _block_diag(lru_wx[l])], axis=-1).astype(BF16),
        ba=lru_ba[l], bx=lru_bx[l], lam=lru_lambda[l],
        glru=out_norm_lru_g[l].reshape(1, LRU_WIDTH), gatt=out_norm_attn_g[l].reshape(1, ATTN_WIDTH),
        w_out=w_out[l].astype(BF16), gff=norm_ff_g[l].reshape(1, D),
        w_up=w_up[l].astype(BF16), w_down=w_down[l].astype(BF16))


def _rope_tables(seq_len, padded_len):
    half = QK_ROPE // 2
    inv_freq = 1.0 / (ROPE_THETA ** (jnp.arange(0, QK_ROPE, 2, dtype=F32) / QK_ROPE))
    ang = jnp.arange(seq_len, dtype=F32)[:, None] * inv_freq[None, :]
    cos, sin = jnp.cos(ang), jnp.sin(ang)
    zeros = jnp.zeros((seq_len, half), F32)
    tail = jnp.zeros((seq_len, HEAD_PAD - QK_HEAD), F32)
    nope0 = jnp.zeros((seq_len, QK_NOPE), F32)
    cos_t = jnp.concatenate([jnp.ones((seq_len, QK_NOPE), F32), cos, cos, tail], axis=1)
    sin_a = jnp.concatenate([nope0, -sin, zeros, tail], axis=1)
    sin_b = jnp.concatenate([nope0, zeros, sin, tail], axis=1)
    pad = ((0, padded_len - seq_len), (0, 0))
    return [jnp.pad(t, pad) for t in (cos_t, sin_a, sin_b)]


def _run_trunk(x, meta_tokens, layers):
    B, S, D = x.shape
    T = S + N_META
    Tp = -(-T // ROW_TILE) * ROW_TILE
    meta = jnp.broadcast_to(meta_tokens.astype(x.dtype)[None], (B, N_META, D))
    h = jnp.concatenate([meta, x, jnp.zeros((B, Tp - T, D), x.dtype)], axis=1)
    tabs = _rope_tables(T, Tp)
    for lw in layers:
        u, gate, q, k, v = _in_proj(h, lw, tabs)
        hf, hb = _lru(u, lw, T)
        o = _attention(q, k, v, T)
        h = _out_ffn(h, hf, hb, gate, o, lw, T)
    return h[:, N_META:T]


def kernel(x_prompt, x_sample, meta_tokens, norm_mix_g, w_in, conv_w, conv_b, lru_wa, lru_ba, lru_wx, lru_bx,
           lru_lambda, q_norm_g, w_uq, kv_norm_g, w_ukv, qk_q_g, qk_k_g, out_norm_lru_g, out_norm_attn_g,
           w_out, norm_ff_g, w_up, w_down):
    params = (norm_mix_g, w_in, conv_w, conv_b, lru_wa, lru_ba, lru_wx, lru_bx, lru_lambda,
              q_norm_g, w_uq, kv_norm_g, w_ukv, qk_q_g, qk_k_g, out_norm_lru_g, out_norm_attn_g,
              w_out, norm_ff_g, w_up, w_down)
    layers = [_layer_weights(l, *params) for l in range(w_in.shape[0])]
    return (_run_trunk(x_prompt, meta_tokens, layers), _run_trunk(x_sample, meta_tokens, layers))
```

```python
import functools

import jax
import jax.numpy as jnp
from jax import lax
from jax.experimental import pallas as pl
from jax.experimental.pallas import tpu as pltpu

N_META = 16
LRU_WIDTH = 512
LRU_BLOCKS = 8
LRU_BLOCK = LRU_WIDTH // LRU_BLOCKS
CONV_WIDTH = 4
CONV_LEFT = 2
LRU_C = 8.0
MLA_HEADS = 8
Q_LORA = 256
KV_LORA = 128
QK_NOPE = 64
QK_ROPE = 32
QK_HEAD = QK_NOPE + QK_ROPE
V_HEAD = 64
ATTN_WIDTH = MLA_HEADS * V_HEAD
ROPE_THETA = 10000.0
EPS = 1e-6
LOG2_E = 1.4426950408889634

LANES = 128
SUBLANES = 8
MXU_DIM = 256
VMEM_LIMIT_BYTES = 56 * 1024 * 1024

HEAD_PAD = LANES
ROW_TILE = 512
KEY_CHUNK = ROW_TILE
SCORE_RING = 4
SCORE_LOOKAHEAD = 2
CHUNKS_PER_TRIP = 8
HEADS_PER_STEP = HEAD_PAD // V_HEAD
V_ROWS = V_HEAD + 2 * SUBLANES
GATE_GROUP = MXU_DIM

F32 = jnp.float32
BF16 = jnp.bfloat16


def _rmsnorm(x, g):
    return x * lax.rsqrt(jnp.mean(x * x, axis=-1, keepdims=True) + EPS) * g


def _dot(a, b):
    return jnp.dot(a, b, preferred_element_type=F32)


def _head_scale(x):
    return lax.rsqrt(jnp.sum(x * x, axis=-1, keepdims=True) * (1.0 / QK_HEAD) + EPS)


def _in_proj_kernel(h_ref, gmix_ref, win_ref, gqn_ref, wuq_ref, wuqs_ref, gkvn_ref, wuk_ref, wuv_ref,
                    gq_ref, gqs_ref, gk_ref, gks_ref, cos_ref, sina_ref, sinb_ref,
                    u_ref, gate_ref, q_ref, k_ref, v_ref):
    hn = _rmsnorm(h_ref[0], gmix_ref[...])
    proj = _dot(hn.astype(BF16), win_ref[...])
    o_gate = LRU_WIDTH
    o_cq = 2 * LRU_WIDTH
    o_ckv = o_cq + Q_LORA
    o_kr = o_ckv + KV_LORA
    u_ref[0] = proj[:, :o_gate]
    gate_ref[0] = proj[:, o_gate:o_cq]
    k_rope = proj[:, o_kr:o_kr + HEAD_PAD]
    qn = _rmsnorm(proj[:, o_cq:o_ckv], gqn_ref[...]).astype(BF16)
    kvn_f = _rmsnorm(proj[:, o_ckv:o_kr], gkvn_ref[...])
    kvn = kvn_f.astype(BF16)
    qf = _dot(qn, wuq_ref[...])
    qsf = _dot(qn, wuqs_ref[...])
    kf = _dot(kvn, wuk_ref[...])
    vt = _dot(wuv_ref[...], kvn_f.T.astype(BF16))
    value_row = lax.broadcasted_iota(jnp.int32, (V_ROWS, vt.shape[1]), 0) < V_HEAD
    cos, sin_a, sin_b = cos_ref[...], sina_ref[...], sinb_ref[...]
    q_cos = cos * gq_ref[...]
    q_sin = (sin_a + sin_b) * gqs_ref[...]
    k_cos = cos * gk_ref[...]
    half = QK_ROPE // 2
    k_swap = (pltpu.roll(k_rope, HEAD_PAD - half, axis=1) * sin_a
              + pltpu.roll(k_rope, half, axis=1) * sin_b) * gks_ref[...]
    scale = QK_HEAD ** -0.5 * LOG2_E
    for hd in range(MLA_HEADS):
        sl = slice(hd * HEAD_PAD, (hd + 1) * HEAD_PAD)
        v_ref[0, hd, 0] = jnp.where(value_row, vt[hd * V_ROWS:(hd + 1) * V_ROWS], 1.0).astype(BF16)
        xq = qf[:, sl]
        q = (xq * q_cos + qsf[:, sl] * q_sin) * (_head_scale(xq) * scale)
        q_ref[0, hd] = q.astype(BF16)
        xk = kf[:, sl] + k_rope
        k_ref[0, hd] = ((xk * k_cos + k_swap) * _head_scale(xk)).astype(BF16)


def _in_proj(h, lw, tabs):
    B, Tp, D = h.shape
    nT = Tp // ROW_TILE
    const = lambda b, j: (0, 0)
    row3 = lambda b, j: (b, j, 0)
    full = lambda a: pl.BlockSpec(a.shape, const)
    weights = [lw['gmix'], lw['w_in'], lw['gqn'], lw['w_uq'], lw['w_uq_swap'], lw['gkvn'], lw['w_uk'],
               lw['w_uv'], lw['gq'], lw['gq_swap'], lw['gk'], lw['gk_swap']]
    tab_spec = pl.BlockSpec((ROW_TILE, HEAD_PAD), lambda b, j: (j, 0))
    head_spec = pl.BlockSpec((1, MLA_HEADS, ROW_TILE, HEAD_PAD), lambda b, j: (b, 0, j, 0))
    return pl.pallas_call(
        _in_proj_kernel,
        grid=(B, nT),
        in_specs=[pl.BlockSpec((1, ROW_TILE, D), row3)] + [full(w) for w in weights] + [tab_spec] * 3,
        out_specs=[pl.BlockSpec((1, ROW_TILE, LRU_WIDTH), row3),
                   pl.BlockSpec((1, ROW_TILE, LRU_WIDTH), row3),
                   head_spec, head_spec,
                   pl.BlockSpec((1, MLA_HEADS, 1, V_ROWS, ROW_TILE), lambda b, j: (b, 0, j, 0, 0))],
        out_shape=[jax.ShapeDtypeStruct((B, Tp, LRU_WIDTH), F32),
                   jax.ShapeDtypeStruct((B, Tp, LRU_WIDTH), F32),
                   jax.ShapeDtypeStruct((B, MLA_HEADS, Tp, HEAD_PAD), BF16),
                   jax.ShapeDtypeStruct((B, MLA_HEADS, Tp, HEAD_PAD), BF16),
                   jax.ShapeDtypeStruct((B, MLA_HEADS, nT, V_ROWS, ROW_TILE), BF16)],
        compiler_params=pltpu.CompilerParams(
            dimension_semantics=("parallel", "parallel"), vmem_limit_bytes=VMEM_LIMIT_BYTES),
        name="in_proj",
    )(h, *weights, *tabs)


def _softplus(z):
    return jnp.maximum(z, 0.0) + jnp.log1p(jnp.exp(-jnp.abs(z)))


def _lru_kernel(uf_ref, ufp_ref, ufn_ref, ub_ref, ubp_ref, ubn_ref,
                cw_ref, cb_ref, wg_ref, ba_ref, bx_ref, lam_ref,
                hf_ref, hb_ref, ext_ref, a_ref, b_ref, carry_ref, *, seq_len, n_tiles):
    R = ROW_TILE
    H = SUBLANES
    j = pl.program_id(1)

    @pl.when(j == 0)
    def _():
        carry_ref[...] = jnp.zeros_like(carry_ref)

    row_in_group = lax.broadcasted_iota(jnp.int32, (R // H, H, LRU_WIDTH), 1)
    row = lax.broadcasted_iota(jnp.int32, (R, 1), 0)
    cw = cw_ref[...]

    for z, (cur, prev, nxt) in enumerate(((uf_ref, ufp_ref, ufn_ref), (ub_ref, ubp_ref, ubn_ref))):
        tile = j if z == 0 else n_tiles - 1 - j
        ext_ref[0:H, :] = jnp.where(tile > 0, prev[0], 0.0)
        ext_ref[H:H + R, :] = cur[0]
        ext_ref[H + R:2 * H + R, :] = jnp.where(tile < n_tiles - 1, nxt[0], 0.0)
        xc = cb_ref[...] + ext_ref[pl.ds(H - CONV_LEFT, R), :] * cw[0:1]
        for tap in range(1, CONV_WIDTH):
            xc = xc + ext_ref[pl.ds(H - CONV_LEFT + tap, R), :] * cw[tap:tap + 1]
        xb = xc.astype(BF16)
        sp = LRU_C * _softplus(-lam_ref[z:z + 1, :])
        valid = (tile * R + row) < seq_len
        for c in range(LRU_WIDTH // GATE_GROUP):
            sl = slice(c * GATE_GROUP, (c + 1) * GATE_GROUP)
            g = _dot(xb[:, sl], wg_ref[z, c])
            r = jax.nn.sigmoid(g[:, :GATE_GROUP] + ba_ref[z:z + 1, sl])
            ig = jax.nn.sigmoid(g[:, GATE_GROUP:] + bx_ref[z:z + 1, sl])
            log_a = -(r * sp[:, sl])
            a_ref[z, :, sl] = jnp.exp(log_a)
            t = jnp.tanh(log_a)
            bb = jnp.sqrt(-2.0 * t / (1.0 - t)) * (ig * xc[:, sl])
            b_ref[z, :, sl] = jnp.where(valid, bb, 0.0)
        a = a_ref[z].reshape(R // H, H, LRU_WIDTH)
        b = b_ref[z].reshape(R // H, H, LRU_WIDTH)
        for s in (1, 2, 4):
            if z == 0:
                m = row_in_group >= s
                shift = s
            else:
                m = row_in_group < H - s
                shift = H - s
            a_s = jnp.where(m, pltpu.roll(a, shift, axis=1), 1.0)
            b_s = jnp.where(m, pltpu.roll(b, shift, axis=1), 0.0)
            b = a * b_s + b
            a = a * a_s
        a_ref[z] = a.reshape(R, LRU_WIDTH)
        b_ref[z] = b.reshape(R, LRU_WIDTH)

    n_groups = R // H

    def step(g, carry):
        cf, cbk = carry
        rf = pl.multiple_of(g * H, H)
        hf = a_ref[0, pl.ds(rf, H), :] * cf + b_ref[0, pl.ds(rf, H), :]
        hf_ref[0, pl.ds(rf, H), :] = hf
        rb = pl.multiple_of((n_groups - 1 - g) * H, H)
        hb = a_ref[1, pl.ds(rb, H), :] * cbk + b_ref[1, pl.ds(rb, H), :]
        hb_ref[0, pl.ds(rb, H), :] = hb
        return hf[H - 1:H, :], hb[0:1, :]

    cf, cbk = lax.fori_loop(0, n_groups, step, (carry_ref[0:1, :], carry_ref[1:2, :]), unroll=4)
    carry_ref[0:1, :] = cf
    carry_ref[1:2, :] = cbk


def _lru(u, lw, seq_len):
    B, Tp, W = u.shape
    R, H = ROW_TILE, SUBLANES
    nT = Tp // R
    rpt = R // H
    last_halo = Tp // H - 1
    fwd = lambda b, j: (b, j, 0)
    bwd = lambda b, j: (b, nT - 1 - j, 0)
    fwd_prev = lambda b, j: (b, jnp.maximum(j * rpt - 1, 0), 0)
    fwd_next = lambda b, j: (b, jnp.minimum((j + 1) * rpt, last_halo), 0)
    bwd_prev = lambda b, j: (b, jnp.maximum((nT - 1 - j) * rpt - 1, 0), 0)
    bwd_next = lambda b, j: (b, jnp.minimum((nT - j) * rpt, last_halo), 0)
    tile = lambda im: pl.BlockSpec((1, R, W), im)
    halo = lambda im: pl.BlockSpec((1, H, W), im)
    weights = [lw['conv_w'], lw['conv_b'], lw['w_gate'], lw['ba'], lw['bx'], lw['lam']]
    full = lambda a: pl.BlockSpec(a.shape, lambda b, j, nd=a.ndim: (0,) * nd)
    return pl.pallas_call(
        functools.partial(_lru_kernel, seq_len=seq_len, n_tiles=nT),
        grid=(B, nT),
        in_specs=[tile(fwd), halo(fwd_prev), halo(fwd_next), tile(bwd), halo(bwd_prev), halo(bwd_next)]
                 + [full(w) for w in weights],
        out_specs=[tile(fwd), tile(bwd)],
        out_shape=[jax.ShapeDtypeStruct((B, Tp, W), F32)] * 2,
        scratch_shapes=[pltpu.VMEM((R + 2 * H, W), F32),
                        pltpu.VMEM((2, R, W), F32),
                        pltpu.VMEM((2, R, W), F32),
                        pltpu.VMEM((H, W), F32)],
        compiler_params=pltpu.CompilerParams(
            dimension_semantics=("parallel", "arbitrary"), vmem_limit_bytes=VMEM_LIMIT_BYTES),
        name="lru",
    )(u, u, u, u, u, u, *weights)


def _attn_kernel(q_ref, k_ref, vt_ref, o_ref, acc_ref, st_ref, *, seq_len):
    tq = q_ref.shape[2]
    n_full = seq_len // KEY_CHUNK
    rem = seq_len - n_full * KEY_CHUNK
    heads = range(HEADS_PER_STEP)
    qts = [q_ref[0, hh].astype(F32).T.astype(BF16) for hh in heads]

    def scores(hh, start, size):
        kc = k_ref[0, hh, pl.ds(start, size), :]
        return _dot(kc, qts[hh])

    def produce(slot, hh, i):
        st = scores(hh, pl.multiple_of(i * KEY_CHUNK, KEY_CHUNK), KEY_CHUNK)
        st_ref[slot, hh] = st
        return jnp.max(st, axis=0, keepdims=True)

    def consume(hh, st, chunk_max, vt, m):
        m_new = jnp.maximum(m, chunk_max)
        alpha = jnp.exp2(m - m_new)
        pt = jnp.exp2(st - m_new).astype(BF16)
        acc_ref[hh] = alpha * acc_ref[hh] + _dot(vt, pt)
        return m_new

    def ring_pass(j, carry, n_produce=CHUNKS_PER_TRIP):
        ms, pending = carry
        ms = list(ms)
        pending = [list(p) for p in pending]
        for c in range(CHUNKS_PER_TRIP):
            i = CHUNKS_PER_TRIP * j + c
            chunk_max = pending.pop(0)
            ahead = []
            for hh in heads:
                if c < n_produce:
                    ahead.append(produce((c + SCORE_LOOKAHEAD) % SCORE_RING, hh, i + SCORE_LOOKAHEAD))
                ms[hh] = consume(hh, st_ref[c % SCORE_RING, hh], chunk_max[hh], vt_ref[0, hh, i], ms[hh])
            if ahead:
                pending.append(ahead)
        return tuple(ms), tuple(tuple(p) for p in pending)

    for hh in heads:
        acc_ref[hh] = jnp.zeros((V_ROWS, tq), F32)
    ms = tuple(jnp.full((1, tq), -jnp.inf, F32) for _ in heads)
    if n_full:
        n_pass = n_full // CHUNKS_PER_TRIP
        carry = (ms, tuple(tuple(produce(c, hh, c) for hh in heads) for c in range(SCORE_LOOKAHEAD)))
        carry = lax.fori_loop(0, n_pass - 1, ring_pass, carry)
        ms, _ = ring_pass(n_pass - 1, carry, n_produce=CHUNKS_PER_TRIP - SCORE_LOOKAHEAD)
    if rem:
        blk, off = divmod(n_full * KEY_CHUNK, ROW_TILE)
        assert off + rem <= ROW_TILE
        for hh in heads:
            st = scores(hh, n_full * KEY_CHUNK, rem)
            consume(hh, st, jnp.max(st, axis=0, keepdims=True), vt_ref[0, hh, blk, :, off:off + rem], ms[hh])
    outs = [acc_ref[hh, 0:V_HEAD, :] / acc_ref[hh, V_HEAD:V_HEAD + 1, :] for hh in heads]
    o_ref[0] = jnp.concatenate(outs, axis=0).T


def _attention(q, k, vt, seq_len):
    B, NH, Tp, _ = q.shape
    assert (seq_len % KEY_CHUNK) % (2 * SUBLANES) == 0 and KEY_CHUNK == ROW_TILE
    assert (seq_len // KEY_CHUNK) % CHUNKS_PER_TRIP == 0 and CHUNKS_PER_TRIP % SCORE_RING == 0
    nq = Tp // ROW_TILE
    return pl.pallas_call(
        functools.partial(_attn_kernel, seq_len=seq_len),
        grid=(B, NH // HEADS_PER_STEP, nq),
        in_specs=[pl.BlockSpec((1, HEADS_PER_STEP, ROW_TILE, HEAD_PAD), lambda b, hp, i: (b, hp, i, 0)),
                  pl.BlockSpec((1, HEADS_PER_STEP, Tp, HEAD_PAD), lambda b, hp, i: (b, hp, 0, 0)),
                  pl.BlockSpec((1, HEADS_PER_STEP, nq, V_ROWS, ROW_TILE), lambda b, hp, i: (b, hp, 0, 0, 0))],
        out_specs=pl.BlockSpec((1, ROW_TILE, HEAD_PAD), lambda b, hp, i: (b, i, hp)),
        out_shape=jax.ShapeDtypeStruct((B, Tp, ATTN_WIDTH), F32),
        scratch_shapes=[pltpu.VMEM((HEADS_PER_STEP, V_ROWS, ROW_TILE), F32),
                        pltpu.VMEM((SCORE_RING, HEADS_PER_STEP, KEY_CHUNK, ROW_TILE), F32)],
        compiler_params=pltpu.CompilerParams(
            dimension_semantics=("parallel", "parallel", "arbitrary"), vmem_limit_bytes=VMEM_LIMIT_BYTES),
        name="attn",
    )(q, k, vt)


def _out_ffn_kernel(h_ref, hf_ref, hb_ref, gate_ref, o_ref, glru_ref, gatt_ref, wout_ref, gff_ref,
                    wup_ref, wdown_ref, out_ref, *, seq_len, ff_chunk):
    y = (hf_ref[0] + hb_ref[0]) * jax.nn.gelu(gate_ref[0])
    n_lru = _rmsnorm(y, glru_ref[...]).astype(BF16)
    n_att = _rmsnorm(o_ref[0], gatt_ref[...]).astype(BF16)
    h1 = h_ref[0] + _dot(n_lru, wout_ref[:LRU_WIDTH, :]) + _dot(n_att, wout_ref[LRU_WIDTH:, :])
    hn = _rmsnorm(h1, gff_ref[...]).astype(BF16)
    out_ref[0] = h1
    for c in range(wup_ref.shape[1] // ff_chunk):
        sl = slice(c * ff_chunk, (c + 1) * ff_chunk)
        t = jnp.maximum(_dot(hn, wup_ref[:, sl]), 0.0)
        out_ref[0] += _dot((t * t).astype(BF16), wdown_ref[sl, :])
    row = pl.program_id(1) * ROW_TILE + lax.broadcasted_iota(jnp.int32, (ROW_TILE, 1), 0)
    out_ref[0] = jnp.where(row < seq_len, out_ref[0], 0.0)


def _out_ffn(h, hf, hb, gate, o, lw, seq_len):
    B, Tp, D = h.shape
    nT = Tp // ROW_TILE
    row3 = lambda b, j: (b, j, 0)
    weights = [lw['glru'], lw['gatt'], lw['w_out'], lw['gff'], lw['w_up'], lw['w_down']]
    full = lambda a: pl.BlockSpec(a.shape, lambda b, j: (0, 0), pipeline_mode=pl.Buffered(1))
    act = lambda w: pl.BlockSpec((1, ROW_TILE, w), row3)
    return pl.pallas_call(
        functools.partial(_out_ffn_kernel, seq_len=seq_len, ff_chunk=4 * MXU_DIM),
        grid=(B, nT),
        in_specs=[act(D), act(LRU_WIDTH), act(LRU_WIDTH), act(LRU_WIDTH), act(ATTN_WIDTH)]
                 + [full(w) for w in weights],
        out_specs=act(D),
        out_shape=jax.ShapeDtypeStruct((B, Tp, D), F32),
        compiler_params=pltpu.CompilerParams(
            dimension_semantics=("parallel", "parallel"), vmem_limit_bytes=VMEM_LIMIT_BYTES),
        name="out_ffn",
    )(h, hf, hb, gate, o, *weights)


def _pad_heads(w, width):
    K = w.shape[0]
    w = w.reshape(K, MLA_HEADS, width)
    return jnp.pad(w, ((0, 0), (0, 0), (0, HEAD_PAD - width))).reshape(K, MLA_HEADS * HEAD_PAD)


def _swap_rope_halves(w):
    K = w.shape[0]
    half = QK_ROPE // 2
    w = w.reshape(K, -1, HEAD_PAD)
    lo, hi = w[:, :, QK_NOPE:QK_NOPE + half], w[:, :, QK_NOPE + half:QK_HEAD]
    out = jnp.concatenate([jnp.zeros_like(w[:, :, :QK_NOPE]), hi, lo, jnp.zeros_like(w[:, :, QK_HEAD:])], axis=2)
    return out.reshape(K, -1)


def _block_diag(w):
    per = GATE_GROUP // LRU_BLOCK
    w = w.reshape(2, LRU_BLOCKS // per, per, LRU_BLOCK, LRU_BLOCK)
    eye = jnp.eye(per, dtype=w.dtype)
    return jnp.einsum('zcgij,gh->zcgihj', w, eye).reshape(2, LRU_BLOCKS // per, GATE_GROUP, GATE_GROUP)


def _layer_weights(l, norm_mix_g, w_in, conv_w, conv_b, lru_wa, lru_ba, lru_wx, lru_bx, lru_lambda,
                   q_norm_g, w_uq, kv_norm_g, w_ukv, qk_q_g, qk_k_g, out_norm_lru_g, out_norm_attn_g,
                   w_out, norm_ff_g, w_up, w_down):
    D = w_in.shape[1]
    o_kr = 2 * LRU_WIDTH + Q_LORA + KV_LORA
    w_in_p = jnp.concatenate(
        [w_in[l][:, :o_kr], jnp.zeros((D, QK_NOPE), F32), w_in[l][:, o_kr:],
         jnp.zeros((D, HEAD_PAD - QK_HEAD), F32)], axis=1)
    w_ukv_h = w_ukv[l].reshape(KV_LORA, MLA_HEADS, QK_NOPE + V_HEAD)
    pad_g = lambda g: jnp.pad(g, (0, HEAD_PAD - QK_HEAD)).reshape(1, HEAD_PAD)
    w_uq_p = _pad_heads(w_uq[l], QK_HEAD)
    return dict(
        gmix=norm_mix_g[l].reshape(1, D), w_in=w_in_p.astype(BF16),
        gqn=q_norm_g[l].reshape(1, Q_LORA), w_uq=w_uq_p.astype(BF16),
        w_uq_swap=_swap_rope_halves(w_uq_p).astype(BF16),
        gkvn=kv_norm_g[l].reshape(1, KV_LORA),
        w_uk=_pad_heads(w_ukv_h[:, :, :QK_NOPE].reshape(KV_LORA, -1), QK_NOPE).astype(BF16),
        w_uv=jnp.pad(w_ukv_h[:, :, QK_NOPE:].transpose(1, 2, 0), ((0, 0), (0, V_ROWS - V_HEAD), (0, 0)))
        .reshape(MLA_HEADS * V_ROWS, KV_LORA).astype(BF16),
        gq=pad_g(qk_q_g[l]), gq_swap=_swap_rope_halves(pad_g(qk_q_g[l])),
        gk=pad_g(qk_k_g[l]), gk_swap=_swap_rope_halves(pad_g(qk_k_g[l])),
        conv_w=conv_w[l], conv_b=conv_b[l].reshape(1, LRU_WIDTH),
        w_gate=jnp.concatenate([_block_diag(lru_wa[l]), _block_diag(lru_wx[l])], axis=-1).astype(BF16),
        ba=lru_ba[l], bx=lru_bx[l], lam=lru_lambda[l],
        glru=out_norm_lru_g[l].reshape(1, LRU_WIDTH), gatt=out_norm_attn_g[l].reshape(1, ATTN_WIDTH),
        w_out=w_out[l].astype(BF16), gff=norm_ff_g[l].reshape(1, D),
        w_up=w_up[l].astype(BF16), w_down=w_down[l].astype(BF16))


def _rope_tables(seq_len, padded_len):
    half = QK_ROPE // 2
    inv_freq = 1.0 / (ROPE_THETA ** (jnp.arange(0, QK_ROPE, 2, dtype=F32) / QK_ROPE))
    ang = jnp.arange(seq_len, dtype=F32)[:, None] * inv_freq[None, :]
    cos, sin = jnp.cos(ang), jnp.sin(ang)
    zeros = jnp.zeros((seq_len, half), F32)
    tail = jnp.zeros((seq_len, HEAD_PAD - QK_HEAD), F32)
    nope0 = jnp.zeros((seq_len, QK_NOPE), F32)
    cos_t = jnp.concatenate([jnp.ones((seq_len, QK_NOPE), F32), cos, cos, tail], axis=1)
    sin_a = jnp.concatenate([nope0, -sin, zeros, tail], axis=1)
    sin_b = jnp.concatenate([nope0, zeros, sin, tail], axis=1)
    pad = ((0, padded_len - seq_len), (0, 0))
    return [jnp.pad(t, pad) for t in (cos_t, sin_a, sin_b)]


def _run_trunk(x, meta_tokens, layers):
    B, S, D = x.shape
    T = S + N_META
    Tp = -(-T // ROW_TILE) * ROW_TILE
    meta = jnp.broadcast_to(meta_tokens.astype(x.dtype)[None], (B, N_META, D))
    h = jnp.concatenate([meta, x, jnp.zeros((B, Tp - T, D), x.dtype)], axis=1)
    tabs = _rope_tables(T, Tp)
    for lw in layers:
        u, gate, q, k, v = _in_proj(h, lw, tabs)
        hf, hb = _lru(u, lw, T)
        o = _attention(q, k, v, T)
        h = _out_ffn(h, hf, hb, gate, o, lw, T)
    return h[:, N_META:T]


def kernel(x_prompt, x_sample, meta_tokens, norm_mix_g, w_in, conv_w, conv_b, lru_wa, lru_ba, lru_wx, lru_bx,
           lru_lambda, q_norm_g, w_uq, kv_norm_g, w_ukv, qk_q_g, qk_k_g, out_norm_lru_g, out_norm_attn_g,
           w_out, norm_ff_g, w_up, w_down):
    params = (norm_mix_g, w_in, conv_w, conv_b, lru_wa, lru_ba, lru_wx, lru_bx, lru_lambda,
              q_norm_g, w_uq, kv_norm_g, w_ukv, qk_q_g, qk_k_g, out_norm_lru_g, out_norm_attn_g,
              w_out, norm_ff_g, w_up, w_down)
    layers = [_layer_weights(l, *params) for l in range(w_in.shape[0])]
    return (_run_trunk(x_prompt, meta_tokens, layers), _run_trunk(x_sample, meta_tokens, layers))
```

```python
import functools

import jax
import jax.numpy as jnp
from jax import lax
from jax.experimental import pallas as pl
from jax.experimental.pallas import tpu as pltpu

N_META = 16
LRU_WIDTH = 512
LRU_BLOCKS = 8
LRU_BLOCK = LRU_WIDTH // LRU_BLOCKS
CONV_WIDTH = 4
CONV_LEFT = 2
LRU_C = 8.0
MLA_HEADS = 8
Q_LORA = 256
KV_LORA = 128
QK_NOPE = 64
QK_ROPE = 32
QK_HEAD = QK_NOPE + QK_ROPE
V_HEAD = 64
ATTN_WIDTH = MLA_HEADS * V_HEAD
ROPE_THETA = 10000.0
EPS = 1e-6
LOG2_E = 1.4426950408889634

LANES = 128
SUBLANES = 8
MXU_DIM = 256
VMEM_LIMIT_BYTES = 56 * 1024 * 1024

HEAD_PAD = LANES
ROW_TILE = 512
KEY_CHUNK = ROW_TILE
SCORE_RING = 4
SCORE_LOOKAHEAD = 2
CHUNKS_PER_TRIP = 8
HEADS_PER_STEP = HEAD_PAD // V_HEAD
V_ROWS = V_HEAD + 2 * SUBLANES
GATE_GROUP = MXU_DIM

F32 = jnp.float32
BF16 = jnp.bfloat16


def _rmsnorm(x, g):
    return x * lax.rsqrt(jnp.mean(x * x, axis=-1, keepdims=True) + EPS) * g


def _dot(a, b):
    return jnp.dot(a, b, preferred_element_type=F32)


def _head_scale(x):
    return lax.rsqrt(jnp.sum(x * x, axis=-1, keepdims=True) * (1.0 / QK_HEAD) + EPS)


def _in_proj_kernel(h_ref, gmix_ref, win_ref, gqn_ref, wuq_ref, wuqs_ref, gkvn_ref, wuk_ref, wuv_ref,
                    gq_ref, gqs_ref, gk_ref, gks_ref, cos_ref, sina_ref, sinb_ref,
                    u_ref, gate_ref, q_ref, k_ref, v_ref):
    hn = _rmsnorm(h_ref[0], gmix_ref[...])
    proj = _dot(hn.astype(BF16), win_ref[...])
    o_gate = LRU_WIDTH
    o_cq = 2 * LRU_WIDTH
    o_ckv = o_cq + Q_LORA
    o_kr = o_ckv + KV_LORA
    u_ref[0] = proj[:, :o_gate]
    gate_ref[0] = proj[:, o_gate:o_cq]
    k_rope = proj[:, o_kr:o_kr + HEAD_PAD]
    qn = _rmsnorm(proj[:, o_cq:o_ckv], gqn_ref[...]).astype(BF16)
    kvn_f = _rmsnorm(proj[:, o_ckv:o_kr], gkvn_ref[...])
    kvn = kvn_f.astype(BF16)
    qf = _dot(qn, wuq_ref[...])
    qsf = _dot(qn, wuqs_ref[...])
    kf = _dot(kvn, wuk_ref[...])
    vt = _dot(wuv_ref[...], kvn_f.T.astype(BF16))
    value_row = lax.broadcasted_iota(jnp.int32, (V_ROWS, vt.shape[1]), 0) < V_HEAD
    cos, sin_a, sin_b = cos_ref[...], sina_ref[...], sinb_ref[...]
    q_cos = cos * gq_ref[...]
    q_sin = (sin_a + sin_b) * gqs_ref[...]
    k_cos = cos * gk_ref[...]
    half = QK_ROPE // 2
    k_swap = (pltpu.roll(k_rope, HEAD_PAD - half, axis=1) * sin_a
              + pltpu.roll(k_rope, half, axis=1) * sin_b) * gks_ref[...]
    scale = QK_HEAD ** -0.5 * LOG2_E
    for hd in range(MLA_HEADS):
        sl = slice(hd * HEAD_PAD, (hd + 1) * HEAD_PAD)
        v_ref[0, hd, 0] = jnp.where(value_row, vt[hd * V_ROWS:(hd + 1) * V_ROWS], 1.0).astype(BF16)
        xq = qf[:, sl]
        q = (xq * q_cos + qsf[:, sl] * q_sin) * (_head_scale(xq) * scale)
        q_ref[0, hd] = q.astype(BF16)
        xk = kf[:, sl] + k_rope
        k_ref[0, hd] = ((xk * k_cos + k_swap) * _head_scale(xk)).astype(BF16)


def _in_proj(h, lw, tabs):
    B, Tp, D = h.shape
    nT = Tp // ROW_TILE
    const = lambda b, j: (0, 0)
    row3 = lambda b, j: (b, j, 0)
    full = lambda a: pl.BlockSpec(a.shape, const)
    weights = [lw['gmix'], lw['w_in'], lw['gqn'], lw['w_uq'], lw['w_uq_swap'], lw['gkvn'], lw['w_uk'],
               lw['w_uv'], lw['gq'], lw['gq_swap'], lw['gk'], lw['gk_swap']]
    tab_spec = pl.BlockSpec((ROW_TILE, HEAD_PAD), lambda b, j: (j, 0))
    head_spec = pl.BlockSpec((1, MLA_HEADS, ROW_TILE, HEAD_PAD), lambda b, j: (b, 0, j, 0))
    return pl.pallas_call(
        _in_proj_kernel,
        grid=(B, nT),
        in_specs=[pl.BlockSpec((1, ROW_TILE, D), row3)] + [full(w) for w in weights] + [tab_spec] * 3,
        out_specs=[pl.BlockSpec((1, ROW_TILE, LRU_WIDTH), row3),
                   pl.BlockSpec((1, ROW_TILE, LRU_WIDTH), row3),
                   head_spec, head_spec,
                   pl.BlockSpec((1, MLA_HEADS, 1, V_ROWS, ROW_TILE), lambda b, j: (b, 0, j, 0, 0))],
        out_shape=[jax.ShapeDtypeStruct((B, Tp, LRU_WIDTH), F32),
                   jax.ShapeDtypeStruct((B, Tp, LRU_WIDTH), F32),
                   jax.ShapeDtypeStruct((B, MLA_HEADS, Tp, HEAD_PAD), BF16),
                   jax.ShapeDtypeStruct((B, MLA_HEADS, Tp, HEAD_PAD), BF16),
                   jax.ShapeDtypeStruct((B, MLA_HEADS, nT, V_ROWS, ROW_TILE), BF16)],
        compiler_params=pltpu.CompilerParams(
            dimension_semantics=("parallel", "parallel"), vmem_limit_bytes=VMEM_LIMIT_BYTES),
        name="in_proj",
    )(h, *weights, *tabs)


def _softplus(z):
    return jnp.maximum(z, 0.0) + jnp.log1p(jnp.exp(-jnp.abs(z)))


def _lru_kernel(uf_ref, ufp_ref, ufn_ref, ub_ref, ubp_ref, ubn_ref,
                cw_ref, cb_ref, wg_ref, ba_ref, bx_ref, lam_ref,
                hf_ref, hb_ref, ext_ref, a_ref, b_ref, carry_ref, *, seq_len, n_tiles):
    R = ROW_TILE
    H = SUBLANES
    j = pl.program_id(1)

    @pl.when(j == 0)
    def _():
        carry_ref[...] = jnp.zeros_like(carry_ref)

    row_in_group = lax.broadcasted_iota(jnp.int32, (R // H, H, LRU_WIDTH), 1)
    row = lax.broadcasted_iota(jnp.int32, (R, 1), 0)
    cw = cw_ref[...]

    for z, (cur, prev, nxt) in enumerate(((uf_ref, ufp_ref, ufn_ref), (ub_ref, ubp_ref, ubn_ref))):
        tile = j if z == 0 else n_tiles - 1 - j
        ext_ref[0:H, :] = jnp.where(tile > 0, prev[0], 0.0)
        ext_ref[H:H + R, :] = cur[0]
        ext_ref[H + R:2 * H + R, :] = jnp.where(tile < n_tiles - 1, nxt[0], 0.0)
        xc = cb_ref[...] + ext_ref[pl.ds(H - CONV_LEFT, R), :] * cw[0:1]
        for tap in range(1, CONV_WIDTH):
            xc = xc + ext_ref[pl.ds(H - CONV_LEFT + tap, R), :] * cw[tap:tap + 1]
        xb = xc.astype(BF16)
        sp = LRU_C * _softplus(-lam_ref[z:z + 1, :])
        valid = (tile * R + row) < seq_len
        for c in range(LRU_WIDTH // GATE_GROUP):
            sl = slice(c * GATE_GROUP, (c + 1) * GATE_GROUP)
            g = _dot(xb[:, sl], wg_ref[z, c])
            r = jax.nn.sigmoid(g[:, :GATE_GROUP] + ba_ref[z:z + 1, sl])
            ig = jax.nn.sigmoid(g[:, GATE_GROUP:] + bx_ref[z:z + 1, sl])
            log_a = -(r * sp[:, sl])
            a_ref[z, :, sl] = jnp.exp(log_a)
            t = jnp.tanh(log_a)
            bb = jnp.sqrt(-2.0 * t / (1.0 - t)) * (ig * xc[:, sl])
            b_ref[z, :, sl] = jnp.where(valid, bb, 0.0)
        a = a_ref[z].reshape(R // H, H, LRU_WIDTH)
        b = b_ref[z].reshape(R // H, H, LRU_WIDTH)
        for s in (1, 2, 4):
            if z == 0:
                m = row_in_group >= s
                shift = s
            else:
                m = row_in_group < H - s
                shift = H - s
            a_s = jnp.where(m, pltpu.roll(a, shift, axis=1), 1.0)
            b_s = jnp.where(m, pltpu.roll(b, shift, axis=1), 0.0)
            b = a * b_s + b
            a = a * a_s
        a_ref[z] = a.reshape(R, LRU_WIDTH)
        b_ref[z] = b.reshape(R, LRU_WIDTH)

    n_groups = R // H

    def step(g, carry):
        cf, cbk = carry
        rf = pl.multiple_of(g * H, H)
        hf = a_ref[0, pl.ds(rf, H), :] * cf + b_ref[0, pl.ds(rf, H), :]
        hf_ref[0, pl.ds(rf, H), :] = hf
        rb = pl.multiple_of((n_groups - 1 - g) * H, H)
        hb = a_ref[1, pl.ds(rb, H), :] * cbk + b_ref[1, pl.ds(rb, H), :]
        hb_ref[0, pl.ds(rb, H), :] = hb
        return hf[H - 1:H, :], hb[0:1, :]

    cf, cbk = lax.fori_loop(0, n_groups, step, (carry_ref[0:1, :], carry_ref[1:2, :]), unroll=4)
    carry_ref[0:1, :] = cf
    carry_ref[1:2, :] = cbk


def _lru(u, lw, seq_len):
    B, Tp, W = u.shape
    R, H = ROW_TILE, SUBLANES
    nT = Tp // R
    rpt = R // H
    last_halo = Tp // H - 1
    fwd = lambda b, j: (b, j, 0)
    bwd = lambda b, j: (b, nT - 1 - j, 0)
    fwd_prev = lambda b, j: (b, jnp.maximum(j * rpt - 1, 0), 0)
    fwd_next = lambda b, j: (b, jnp.minimum((j + 1) * rpt, last_halo), 0)
    bwd_prev = lambda b, j: (b, jnp.maximum((nT - 1 - j) * rpt - 1, 0), 0)
    bwd_next = lambda b, j: (b, jnp.minimum((nT - j) * rpt, last_halo), 0)
    tile = lambda im: pl.BlockSpec((1, R, W), im)
    halo = lambda im: pl.BlockSpec((1, H, W), im)
    weights = [lw['conv_w'], lw['conv_b'], lw['w_gate'], lw['ba'], lw['bx'], lw['lam']]
    full = lambda a: pl.BlockSpec(a.shape, lambda b, j, nd=a.ndim: (0,) * nd)
    return pl.pallas_call(
        functools.partial(_lru_kernel, seq_len=seq_len, n_tiles=nT),
        grid=(B, nT),
        in_specs=[tile(fwd), halo(fwd_prev), halo(fwd_next), tile(bwd), halo(bwd_prev), halo(bwd_next)]
                 + [full(w) for w in weights],
        out_specs=[tile(fwd), tile(bwd)],
        out_shape=[jax.ShapeDtypeStruct((B, Tp, W), F32)] * 2,
        scratch_shapes=[pltpu.VMEM((R + 2 * H, W), F32),
                        pltpu.VMEM((2, R, W), F32),
                        pltpu.VMEM((2, R, W), F32),
                        pltpu.VMEM((H, W), F32)],
        compiler_params=pltpu.CompilerParams(
            dimension_semantics=("parallel", "arbitrary"), vmem_limit_bytes=VMEM_LIMIT_BYTES),
        name="lru",
    )(u, u, u, u, u, u, *weights)


def _attn_kernel(q_ref, k_ref, vt_ref, o_ref, acc_ref, st_ref, *, seq_len):
    tq = q_ref.shape[2]
    n_full = seq_len // KEY_CHUNK
    rem = seq_len - n_full * KEY_CHUNK
    heads = range(HEADS_PER_STEP)
    qts = [q_ref[0, hh].astype(F32).T.astype(BF16) for hh in heads]

    def scores(hh, start, size):
        kc = k_ref[0, hh, pl.ds(start, size), :]
        return _dot(kc, qts[hh])

    def produce(slot, hh, i):
        st = scores(hh, pl.multiple_of(i * KEY_CHUNK, KEY_CHUNK), KEY_CHUNK)
        st_ref[slot, hh] = st
        return jnp.max(st, axis=0, keepdims=True)

    def consume(hh, st, chunk_max, vt, m):
        m_new = jnp.maximum(m, chunk_max)
        alpha = jnp.exp2(m - m_new)
        pt = jnp.exp2(st - m_new).astype(BF16)
        acc_ref[hh] = alpha * acc_ref[hh] + _dot(vt, pt)
        return m_new

    def ring_pass(j, carry, n_produce=CHUNKS_PER_TRIP):
        ms, pending = carry
        ms = list(ms)
        pending = [list(p) for p in pending]
        for c in range(CHUNKS_PER_TRIP):
            i = CHUNKS_PER_TRIP * j + c
            chunk_max = pending.pop(0)
            ahead = []
            for hh in heads:
                if c < n_produce:
                    ahead.append(produce((c + SCORE_LOOKAHEAD) % SCORE_RING, hh, i + SCORE_LOOKAHEAD))
                ms[hh] = consume(hh, st_ref[c % SCORE_RING, hh], chunk_max[hh], vt_ref[0, hh, i], ms[hh])
            if ahead:
                pending.append(ahead)
        return tuple(ms), tuple(tuple(p) for p in pending)

    n_pass = n_full // CHUNKS_PER_TRIP
    for hh in heads:
        acc_ref[hh] = jnp.zeros((V_ROWS, tq), F32)
    ms = tuple(jnp.full((1, tq), -jnp.inf, F32) for _ in heads)
    tail_st = [scores(hh, n_full * KEY_CHUNK, rem) for hh in heads] if rem else []
    pending = tuple(tuple(produce(c, hh, c) for hh in heads) for c in range(SCORE_LOOKAHEAD)) if n_full else ()
    if rem:
        blk, off = divmod(n_full * KEY_CHUNK, ROW_TILE)
        assert off + rem <= ROW_TILE
        ms = tuple(consume(hh, tail_st[hh], jnp.max(tail_st[hh], axis=0, keepdims=True),
                           vt_ref[0, hh, blk, :, off:off + rem], ms[hh]) for hh in heads)
    if n_full:
        carry = lax.fori_loop(0, n_pass - 1, ring_pass, (ms, pending))
        ring_pass(n_pass - 1, carry, n_produce=CHUNKS_PER_TRIP - SCORE_LOOKAHEAD)
    outs = [acc_ref[hh, 0:V_HEAD, :] / acc_ref[hh, V_HEAD:V_HEAD + 1, :] for hh in heads]
    o_ref[0] = jnp.concatenate(outs, axis=0).T


def _attention(q, k, vt, seq_len):
    B, NH, Tp, _ = q.shape
    assert (seq_len % KEY_CHUNK) % (2 * SUBLANES) == 0 and KEY_CHUNK == ROW_TILE
    assert (seq_len // KEY_CHUNK) % CHUNKS_PER_TRIP == 0 and CHUNKS_PER_TRIP % SCORE_RING == 0
    nq = Tp // ROW_TILE
    return pl.pallas_call(
        functools.partial(_attn_kernel, seq_len=seq_len),
        grid=(B, NH // HEADS_PER_STEP, nq),
        in_specs=[pl.BlockSpec((1, HEADS_PER_STEP, ROW_TILE, HEAD_PAD), lambda b, hp, i: (b, hp, i, 0)),
                  pl.BlockSpec((1, HEADS_PER_STEP, Tp, HEAD_PAD), lambda b, hp, i: (b, hp, 0, 0)),
                  pl.BlockSpec((1, HEADS_PER_STEP, nq, V_ROWS, ROW_TILE), lambda b, hp, i: (b, hp, 0, 0, 0))],
        out_specs=pl.BlockSpec((1, ROW_TILE, HEAD_PAD), lambda b, hp, i: (b, i, hp)),
        out_shape=jax.ShapeDtypeStruct((B, Tp, ATTN_WIDTH), F32),
        scratch_shapes=[pltpu.VMEM((HEADS_PER_STEP, V_ROWS, ROW_TILE), F32),
                        pltpu.VMEM((SCORE_RING, HEADS_PER_STEP, KEY_CHUNK, ROW_TILE), F32)],
        compiler_params=pltpu.CompilerParams(
            dimension_semantics=("parallel", "parallel", "arbitrary"), vmem_limit_bytes=VMEM_LIMIT_BYTES),
        name="attn",
    )(q, k, vt)


def _out_ffn_kernel(h_ref, hf_ref, hb_ref, gate_ref, o_ref, glru_ref, gatt_ref, wout_ref, gff_ref,
                    wup_ref, wdown_ref, out_ref, *, seq_len, ff_chunk):
    y = (hf_ref[0] + hb_ref[0]) * jax.nn.gelu(gate_ref[0])
    n_lru = _rmsnorm(y, glru_ref[...]).astype(BF16)
    n_att = _rmsnorm(o_ref[0], gatt_ref[...]).astype(BF16)
    h1 = h_ref[0] + _dot(n_lru, wout_ref[:LRU_WIDTH, :]) + _dot(n_att, wout_ref[LRU_WIDTH:, :])
    hn = _rmsnorm(h1, gff_ref[...]).astype(BF16)
    out_ref[0] = h1
    for c in range(wup_ref.shape[1] // ff_chunk):
        sl = slice(c * ff_chunk, (c + 1) * ff_chunk)
        t = jnp.maximum(_dot(hn, wup_ref[:, sl]), 0.0)
        out_ref[0] += _dot((t * t).astype(BF16), wdown_ref[sl, :])
    row = pl.program_id(1) * ROW_TILE + lax.broadcasted_iota(jnp.int32, (ROW_TILE, 1), 0)
    out_ref[0] = jnp.where(row < seq_len, out_ref[0], 0.0)


def _out_ffn(h, hf, hb, gate, o, lw, seq_len):
    B, Tp, D = h.shape
    nT = Tp // ROW_TILE
    row3 = lambda b, j: (b, j, 0)
    weights = [lw['glru'], lw['gatt'], lw['w_out'], lw['gff'], lw['w_up'], lw['w_down']]
    full = lambda a: pl.BlockSpec(a.shape, lambda b, j: (0, 0), pipeline_mode=pl.Buffered(1))
    act = lambda w: pl.BlockSpec((1, ROW_TILE, w), row3)
    return pl.pallas_call(
        functools.partial(_out_ffn_kernel, seq_len=seq_len, ff_chunk=4 * MXU_DIM),
        grid=(B, nT),
        in_specs=[act(D), act(LRU_WIDTH), act(LRU_WIDTH), act(LRU_WIDTH), act(ATTN_WIDTH)]
                 + [full(w) for w in weights],
        out_specs=act(D),
        out_shape=jax.ShapeDtypeStruct((B, Tp, D), F32),
        compiler_params=pltpu.CompilerParams(
            dimension_semantics=("parallel", "parallel"), vmem_limit_bytes=VMEM_LIMIT_BYTES),
        name="out_ffn",
    )(h, hf, hb, gate, o, *weights)


def _pad_heads(w, width):
    K = w.shape[0]
    w = w.reshape(K, MLA_HEADS, width)
    return jnp.pad(w, ((0, 0), (0, 0), (0, HEAD_PAD - width))).reshape(K, MLA_HEADS * HEAD_PAD)


def _swap_rope_halves(w):
    K = w.shape[0]
    half = QK_ROPE // 2
    w = w.reshape(K, -1, HEAD_PAD)
    lo, hi = w[:, :, QK_NOPE:QK_NOPE + half], w[:, :, QK_NOPE + half:QK_HEAD]
    out = jnp.concatenate([jnp.zeros_like(w[:, :, :QK_NOPE]), hi, lo, jnp.zeros_like(w[:, :, QK_HEAD:])], axis=2)
    return out.reshape(K, -1)


def _block_diag(w):
    per = GATE_GROUP // LRU_BLOCK
    w = w.reshape(2, LRU_BLOCKS // per, per, LRU_BLOCK, LRU_BLOCK)
    eye = jnp.eye(per, dtype=w.dtype)
    return jnp.einsum('zcgij,gh->zcgihj', w, eye).reshape(2, LRU_BLOCKS // per, GATE_GROUP, GATE_GROUP)


def _layer_weights(l, norm_mix_g, w_in, conv_w, conv_b, lru_wa, lru_ba, lru_wx, lru_bx, lru_lambda,
                   q_norm_g, w_uq, kv_norm_g, w_ukv, qk_q_g, qk_k_g, out_norm_lru_g, out_norm_attn_g,
                   w_out, norm_ff_g, w_up, w_down):
    D = w_in.shape[1]
    o_kr = 2 * LRU_WIDTH + Q_LORA + KV_LORA
    w_in_p = jnp.concatenate(
        [w_in[l][:, :o_kr], jnp.zeros((D, QK_NOPE), F32), w_in[l][:, o_kr:],
         jnp.zeros((D, HEAD_PAD - QK_HEAD), F32)], axis=1)
    w_ukv_h = w_ukv[l].reshape(KV_LORA, MLA_HEADS, QK_NOPE + V_HEAD)
    pad_g = lambda g: jnp.pad(g, (0, HEAD_PAD - QK_HEAD)).reshape(1, HEAD_PAD)
    w_uq_p = _pad_heads(w_uq[l], QK_HEAD)
    return dict(
        gmix=norm_mix_g[l].reshape(1, D), w_in=w_in_p.astype(BF16),
        gqn=q_norm_g[l].reshape(1, Q_LORA), w_uq=w_uq_p.astype(BF16),
        w_uq_swap=_swap_rope_halves(w_uq_p).astype(BF16),
        gkvn=kv_norm_g[l].reshape(1, KV_LORA),
        w_uk=_pad_heads(w_ukv_h[:, :, :QK_NOPE].reshape(KV_LORA, -1), QK_NOPE).astype(BF16),
        w_uv=jnp.pad(w_ukv_h[:, :, QK_NOPE:].transpose(1, 2, 0), ((0, 0), (0, V_ROWS - V_HEAD), (0, 0)))
        .reshape(MLA_HEADS * V_ROWS, KV_LORA).astype(BF16),
        gq=pad_g(qk_q_g[l]), gq_swap=_swap_rope_halves(pad_g(qk_q_g[l])),
        gk=pad_g(qk_k_g[l]), gk_swap=_swap_rope_halves(pad_g(qk_k_g[l])),
        conv_w=conv_w[l], conv_b=conv_b[l].reshape(1, LRU_WIDTH),
        w_gate=jnp.concatenate([_block_diag(lru_wa[l]), _block_diag(lru_wx[l])], axis=-1).astype(BF16),
        ba=lru_ba[l], bx=lru_bx[l], lam=lru_lambda[l],
        glru=out_norm_lru_g[l].reshape(1, LRU_WIDTH), gatt=out_norm_attn_g[l].reshape(1, ATTN_WIDTH),
        w_out=w_out[l].astype(BF16), gff=norm_ff_g[l].reshape(1, D),
        w_up=w_up[l].astype(BF16), w_down=w_down[l].astype(BF16))


def _rope_tables(seq_len, padded_len):
    half = QK_ROPE // 2
    inv_freq = 1.0 / (ROPE_THETA ** (jnp.arange(0, QK_ROPE, 2, dtype=F32) / QK_ROPE))
    ang = jnp.arange(seq_len, dtype=F32)[:, None] * inv_freq[None, :]
    cos, sin = jnp.cos(ang), jnp.sin(ang)
    zeros = jnp.zeros((seq_len, half), F32)
    tail = jnp.zeros((seq_len, HEAD_PAD - QK_HEAD), F32)
    nope0 = jnp.zeros((seq_len, QK_NOPE), F32)
    cos_t = jnp.concatenate([jnp.ones((seq_len, QK_NOPE), F32), cos, cos, tail], axis=1)
    sin_a = jnp.concatenate([nope0, -sin, zeros, tail], axis=1)
    sin_b = jnp.concatenate([nope0, zeros, sin, tail], axis=1)
    pad = ((0, padded_len - seq_len), (0, 0))
    return [jnp.pad(t, pad) for t in (cos_t, sin_a, sin_b)]


def _run_trunk(x, meta_tokens, layers):
    B, S, D = x.shape
    T = S + N_META
    Tp = -(-T // ROW_TILE) * ROW_TILE
    meta = jnp.broadcast_to(meta_tokens.astype(x.dtype)[None], (B, N_META, D))
    h = jnp.concatenate([meta, x, jnp.zeros((B, Tp - T, D), x.dtype)], axis=1)
    tabs = _rope_tables(T, Tp)
    for lw in layers:
        u, gate, q, k, v = _in_proj(h, lw, tabs)
        hf, hb = _lru(u, lw, T)
        o = _attention(q, k, v, T)
        h = _out_ffn(h, hf, hb, gate, o, lw, T)
    return h[:, N_META:T]


def kernel(x_prompt, x_sample, meta_tokens, norm_mix_g, w_in, conv_w, conv_b, lru_wa, lru_ba, lru_wx, lru_bx,
           lru_lambda, q_norm_g, w_uq, kv_norm_g, w_ukv, qk_q_g, qk_k_g, out_norm_lru_g, out_norm_attn_g,
           w_out, norm_ff_g, w_up, w_down):
    params = (norm_mix_g, w_in, conv_w, conv_b, lru_wa, lru_ba, lru_wx, lru_bx, lru_lambda,
              q_norm_g, w_uq, kv_norm_g, w_ukv, qk_q_g, qk_k_g, out_norm_lru_g, out_norm_attn_g,
              w_out, norm_ff_g, w_up, w_down)
    layers = [_layer_weights(l, *params) for l in range(w_in.shape[0])]
    return (_run_trunk(x_prompt, meta_tokens, layers), _run_trunk(x_sample, meta_tokens, layers))
```
